```python
import math, functools
import jax, jax.numpy as jnp
from jax import lax
import numpy as np

D_MODEL = 1024
BATCH = 4
SEQ = 8192
DEPTH = 2

GRID_W = 64
CTX_LEN = 256
HEAD_DIM = 64
N_FREQ = HEAD_DIM // 4
ROPE_THETA = 10000.0
Q_BLOCK = 128
A_HEADS = 8
A_KV_HEADS = 2
LRU_WIDTH = 512
LRU_BLOCKS = 8
LRU_BW = LRU_WIDTH // LRU_BLOCKS
CONV_WIDTH = 4
LRU_C = 8.0
C_HEADS = 4
C_VDIM = 2 * HEAD_DIM
N_BRANCH = 3
BRANCH_W = 512
N_EXPERTS = 16
EC_FACTOR = 2
D_EXPERT = 2752
DN_ALPHA = (2 * DEPTH) ** 0.25
DN_BETA = (8 * DEPTH) ** -0.25
LN_EPS = 1e-5
RMS_EPS = 1e-6

A_Q = A_HEADS * HEAD_DIM
A_KV = A_KV_HEADS * HEAD_DIM
C_QK = C_HEADS * 2 * HEAD_DIM
C_V = C_HEADS * C_VDIM
SPLITS = (A_Q, A_KV, A_KV, LRU_WIDTH, LRU_WIDTH, C_QK, C_QK, C_V, N_BRANCH * D_MODEL)
SPLIT_IDX = tuple(sum(SPLITS[:i + 1]) for i in range(len(SPLITS) - 1))
D_IN = sum(SPLITS)

kernel_name = "hybrid_gqa_rglru_diffattn_ecmoe_dit"


def layer_norm(x, g, b):
    xf = x.astype(jnp.float32)
    mu = jnp.mean(xf, -1, keepdims=True)
    var = jnp.mean(jnp.square(xf - mu), -1, keepdims=True)
    return ((xf - mu) * lax.rsqrt(var + LN_EPS)).astype(x.dtype) * g + b


def rms_norm(x, g):
    xf = x.astype(jnp.float32)
    return (xf * lax.rsqrt(jnp.mean(jnp.square(xf), -1, keepdims=True) + RMS_EPS)).astype(x.dtype) * g


def rope_tables(seq, dtype):
    rows = seq // GRID_W
    row = jnp.repeat(jnp.arange(rows, dtype=jnp.int32), GRID_W)
    col = jnp.tile(jnp.arange(GRID_W, dtype=jnp.int32), rows)
    inv = ROPE_THETA ** (-jnp.arange(N_FREQ, dtype=jnp.float32) / N_FREQ)
    ang = jnp.stack([row, col], -1).astype(jnp.float32)[..., None] * inv
    return jnp.cos(ang).astype(dtype), jnp.sin(ang).astype(dtype)


def apply_rope(x, cos, sin):
    b, s, h, _ = x.shape
    xr = x.reshape(b, s, h, 2, 2, N_FREQ)
    x1, x2 = xr[..., 0, :], xr[..., 1, :]
    cs, sn = cos[None, :, None], sin[None, :, None]
    out = jnp.stack([x1 * cs - x2 * sn, x2 * cs + x1 * sn], axis=-2)
    return out.reshape(x.shape)


def gqa_chunk(q, k, v):
    s = jnp.einsum("bqkgd,bnkd->bkgqn", q, k).astype(jnp.float32)
    p = jax.nn.softmax(s, axis=-1).astype(v.dtype)
    return jnp.einsum("bkgqn,bnkd->bqkgd", p, v)


def diff_chunk(q, k, v, lam):
    s = jnp.einsum("bqhjd,bnhjd->bhjqn", q, k).astype(jnp.float32)
    p = jax.nn.softmax(s, axis=-1)
    a = (p[:, :, 0] - lam * p[:, :, 1]).astype(v.dtype)
    return jnp.einsum("bhqn,bnhe->bqhe", a, v)


def blockwise(chunk_fn, q, *kv):
    b, s = q.shape[:2]
    qb = jnp.moveaxis(q.reshape((b, s // Q_BLOCK, Q_BLOCK) + q.shape[2:]), 1, 0)
    o = lax.map(lambda qi: chunk_fn(qi, *kv), qb)
    o = jnp.moveaxis(o, 0, 1)
    return o.reshape((b, s) + o.shape[3:])


def depthwise_conv(x, w, bias):
    left = CONV_WIDTH // 2
    y = lax.conv_general_dilated(
        x, w[:, None, :], window_strides=(1,), padding=[(left, CONV_WIDTH - 1 - left)],
        dimension_numbers=("NWC", "WIO", "NWC"), feature_group_count=x.shape[-1])
    return y + bias


def lru_gates(u, w_a, b_a, w_x, b_x, lam):
    ub = u.reshape(u.shape[:-1] + (LRU_BLOCKS, LRU_BW))
    r = jax.nn.sigmoid(jnp.einsum("btgi,gij->btgj", ub, w_a).reshape(u.shape) + b_a)
    i = jax.nn.sigmoid(jnp.einsum("btgi,gij->btgj", ub, w_x).reshape(u.shape) + b_x)
    log_a = -LRU_C * r.astype(jnp.float32) * jax.nn.softplus(-lam.astype(jnp.float32))
    a = jnp.exp(log_a)
    bterm = jnp.sqrt(-jnp.expm1(2.0 * log_a)) * (i * u).astype(jnp.float32)
    return a, bterm


def linear_scan(a, b, h0, reverse):
    if reverse:
        a, b = jnp.flip(a, 1), jnp.flip(b, 1)
    comb = lambda l, r: (l[0] * r[0], r[0] * l[1] + r[1])
    acum, hz = lax.associative_scan(comb, (a, b), axis=1)
    h = hz + acum * h0[:, None, :]
    return jnp.flip(h, 1) if reverse else h


def rglru(x, xc, conv_w, conv_b, w_a, b_a, w_x, b_x, lam, need_ctx):
    u = depthwise_conv(x, conv_w, conv_b)
    uc = depthwise_conv(xc, conv_w, conv_b)
    h0 = jnp.zeros((x.shape[0], x.shape[-1]), jnp.float32)
    hs_l, hs_c = [], []
    for d, rev in ((0, False), (1, True)):
        a_c, b_c = lru_gates(uc, w_a[d], b_a[d], w_x[d], b_x[d], lam[d])
        h_c = linear_scan(a_c, b_c, h0, rev)
        a_l, b_l = lru_gates(u, w_a[d], b_a[d], w_x[d], b_x[d], lam[d])
        hs_l.append(linear_scan(a_l, b_l, h_c[:, 0] if rev else h_c[:, -1], rev))
        hs_c.append(h_c)
    y = (hs_l[0] + hs_l[1]).astype(x.dtype)
    yc = (hs_c[0] + hs_c[1]).astype(x.dtype) if need_ctx else None
    return y, yc


def diff_post(y, g, lam_init):
    y = rms_norm(y, g) * (1.0 - lam_init)
    return y.reshape(y.shape[:2] + (-1,))


def merge_branches(ya, yb, yc, gm, w_branch, w_out):
    br = jnp.stack([ya, yb, yc], axis=-2)
    p = jnp.einsum("btni,nid->btnd", br, w_branch)
    gates = jax.nn.sigmoid(gm.reshape(gm.shape[:-1] + (N_BRANCH, D_MODEL)))
    return jnp.sum(gates * p, axis=-2) @ w_out


def token_mixer(h, hc, cos, sin, w_in, a_q_norm, a_k_norm, conv_w, conv_b, lw_a, lb_a, lw_x, lb_x,
                llam, diff_lambda, diff_subln, w_branch, w_out, lam_init, need_ctx):
    b, s, _ = h.shape
    lc = hc.shape[1]
    grp = A_HEADS // A_KV_HEADS
    scale = HEAD_DIM ** -0.5
    qa, ka, va, xb, gb, qc, kc, vc, gm = jnp.split(h @ w_in, SPLIT_IDX, axis=-1)
    qa_c, ka_c, va_c, xb_c, gb_c, qc_c, kc_c, vc_c, gm_c = jnp.split(hc @ w_in, SPLIT_IDX, axis=-1)

    qa = apply_rope(rms_norm(qa.reshape(b, s, A_HEADS, HEAD_DIM), a_q_norm), cos, sin) * scale
    ka = apply_rope(rms_norm(ka.reshape(b, s, A_KV_HEADS, HEAD_DIM), a_k_norm), cos, sin)
    ka_c = rms_norm(ka_c.reshape(b, lc, A_KV_HEADS, HEAD_DIM), a_k_norm)
    va_c = va_c.reshape(b, lc, A_KV_HEADS, HEAD_DIM)
    ka_all = jnp.concatenate([ka_c, ka], axis=1)
    va_all = jnp.concatenate([va_c, va.reshape(b, s, A_KV_HEADS, HEAD_DIM)], axis=1)
    ya = blockwise(gqa_chunk, qa.reshape(b, s, A_KV_HEADS, grp, HEAD_DIM), ka_all, va_all)
    ya = ya.reshape(b, s, A_Q)

    hb, hb_c = rglru(xb, xb_c, conv_w, conv_b, lw_a, lb_a, lw_x, lb_x, llam, need_ctx)
    yb = hb * jax.nn.gelu(gb)

    lq1, lk1, lq2, lk2 = diff_lambda.astype(jnp.float32)
    lam = jnp.exp(jnp.sum(lq1 * lk1)) - jnp.exp(jnp.sum(lq2 * lk2)) + lam_init
    qc = apply_rope(qc.reshape(b, s, 2 * C_HEADS, HEAD_DIM), cos, sin).reshape(b, s, C_HEADS, 2, HEAD_DIM) * scale
    kc = apply_rope(kc.reshape(b, s, 2 * C_HEADS, HEAD_DIM), cos, sin).reshape(b, s, C_HEADS, 2, HEAD_DIM)
    kc_c = kc_c.reshape(b, lc, C_HEADS, 2, HEAD_DIM)
    vc_c = vc_c.reshape(b, lc, C_HEADS, C_VDIM)
    kc_all = jnp.concatenate([kc_c, kc], axis=1)
    vc_all = jnp.concatenate([vc_c, vc.reshape(b, s, C_HEADS, C_VDIM)], axis=1)
    yc = diff_post(blockwise(functools.partial(diff_chunk, lam=lam), qc, kc_all, vc_all), diff_subln, lam_init)

    o = merge_branches(ya, yb, yc, gm, w_branch, w_out)
    if not need_ctx:
        return o, None

    qa_c = rms_norm(qa_c.reshape(b, lc, A_HEADS, HEAD_DIM), a_q_norm) * scale
    ya_c = gqa_chunk(qa_c.reshape(b, lc, A_KV_HEADS, grp, HEAD_DIM), ka_c, va_c).reshape(b, lc, A_Q)
    yb_c = hb_c * jax.nn.gelu(gb_c)
    qc_c = qc_c.reshape(b, lc, C_HEADS, 2, HEAD_DIM) * scale
    yc_c = diff_post(diff_chunk(qc_c, kc_c, vc_c, lam), diff_subln, lam_init)
    oc = merge_branches(ya_c, yb_c, yc_c, gm_c, w_branch, w_out)
    return o, oc


def expert_choice(h, w_router, w_gate, w_up, w_down):
    b, n, d = h.shape
    cap = EC_FACTOR * n // N_EXPERTS
    aff = jax.nn.softmax((h @ w_router).astype(jnp.float32), axis=-1)
    vals, idx = lax.top_k(jnp.swapaxes(aff, 1, 2), cap)
    flat_idx = idx.reshape(b, -1)
    xg = jax.vmap(lambda hb, ib: hb[ib])(h, flat_idx).reshape(b, N_EXPERTS, cap, d)
    act = jax.nn.silu(jnp.einsum("becd,edf->becf", xg, w_gate)) * jnp.einsum("becd,edf->becf", xg, w_up)
    out = jnp.einsum("becf,efd->becd", act, w_down) * vals[..., None].astype(h.dtype)
    return jax.vmap(lambda ob, ib: jnp.zeros((n, d), h.dtype).at[ib].add(ob))(out.reshape(b, -1, d), flat_idx)


def setup_inputs(seed: int = 0) -> dict:
    key = jax.random.key(seed)
    ks = jax.random.split(key, 32)
    L, D, E, F = DEPTH, D_MODEL, N_EXPERTS, D_EXPERT
    nrm = lambda k, shape, sd: jax.random.normal(k, shape, jnp.float32) * sd
    u = jax.random.uniform(ks[13], (L, 2, LRU_WIDTH), jnp.float32, 0.9, 0.999)
    a = u ** (1.0 / LRU_C)
    return {
        "x": nrm(ks[0], (BATCH, SEQ, D), 1.0),
        "c": nrm(ks[1], (BATCH, D), 1.0),
        "ctx": nrm(ks[2], (BATCH, CTX_LEN, D), 1.0),
        "c_ctx": nrm(ks[3], (D,), 1.0),
        "w_ada": nrm(ks[4], (L, D, 6 * D), 0.5 * D ** -0.5),
        "b_ada": nrm(ks[5], (L, 6 * D), 0.02),
        "w_in": nrm(ks[6], (L, D, D_IN), D ** -0.5),
        "a_q_norm": 1.0 + nrm(ks[7], (L, HEAD_DIM), 0.02),
        "a_k_norm": 1.0 + nrm(ks[8], (L, HEAD_DIM), 0.02),
        "lru_conv_w": nrm(ks[9], (L, CONV_WIDTH, LRU_WIDTH), CONV_WIDTH ** -0.5),
        "lru_conv_b": nrm(ks[10], (L, LRU_WIDTH), 0.02),
        "lru_w_a": nrm(ks[11], (L, 2, LRU_BLOCKS, LRU_BW, LRU_BW), LRU_BW ** -0.5),
        "lru_b_a": nrm(ks[12], (L, 2, LRU_WIDTH), 0.02),
        "lru_w_x": nrm(ks[14], (L, 2, LRU_BLOCKS, LRU_BW, LRU_BW), LRU_BW ** -0.5),
        "lru_b_x": nrm(ks[15], (L, 2, LRU_WIDTH), 0.02),
        "lru_lambda": jnp.log(a) - jnp.log1p(-a),
        "diff_lambda": nrm(ks[16], (L, 4, HEAD_DIM), 0.1),
        "diff_subln": 1.0 + nrm(ks[17], (L, C_VDIM), 0.02),
        "w_branch": nrm(ks[18], (L, N_BRANCH, BRANCH_W, D), BRANCH_W ** -0.5 * DN_BETA),
        "w_out": nrm(ks[19], (L, D, D), D ** -0.5 * DN_BETA),
        "ln1_g": 1.0 + nrm(ks[20], (L, D), 0.02),
        "ln1_b": nrm(ks[21], (L, D), 0.02),
        "w_router": nrm(ks[22], (L, D, E), D ** -0.5),
        "w_gate": nrm(ks[23], (L, E, D, F), D ** -0.5),
        "w_up": nrm(ks[24], (L, E, D, F), D ** -0.5),
        "w_down": nrm(ks[25], (L, E, F, D), F ** -0.5 * DN_BETA),
        "ln2_g": 1.0 + nrm(ks[26], (L, D), 0.02),
        "ln2_b": nrm(ks[27], (L, D), 0.02),
    }


def reference(x, c, ctx, c_ctx, w_ada, b_ada, w_in, a_q_norm, a_k_norm, lru_conv_w, lru_conv_b,
              lru_w_a, lru_b_a, lru_w_x, lru_b_x, lru_lambda, diff_lambda, diff_subln, w_branch, w_out,
              ln1_g, ln1_b, w_router, w_gate, w_up, w_down, ln2_g, ln2_b):
    cos, sin = rope_tables(x.shape[1], x.dtype)
    for l in range(DEPTH):
        need_ctx = l < DEPTH - 1
        lam_init = 0.8 - 0.6 * math.exp(-0.3 * l)
        mod = jax.nn.silu(c) @ w_ada[l] + b_ada[l]
        mod_c = jax.nn.silu(c_ctx) @ w_ada[l] + b_ada[l]
        sh1, sc1, g1, sh2, sc2, g2 = jnp.split(mod[:, None, :], 6, axis=-1)
        sh1c, sc1c, g1c, sh2c, sc2c, g2c = jnp.split(mod_c, 6, axis=-1)

        o, oc = token_mixer(x * (1.0 + sc1) + sh1, ctx * (1.0 + sc1c) + sh1c, cos, sin,
                            w_in[l], a_q_norm[l], a_k_norm[l], lru_conv_w[l], lru_conv_b[l],
                            lru_w_a[l], lru_b_a[l], lru_w_x[l], lru_b_x[l], lru_lambda[l],
                            diff_lambda[l], diff_subln[l], w_branch[l], w_out[l], lam_init, need_ctx)
        x = layer_norm(DN_ALPHA * x + g1 * o, ln1_g[l], ln1_b[l])
        moe = expert_choice(x * (1.0 + sc2) + sh2, w_router[l], w_gate[l], w_up[l], w_down[l])
        x = layer_norm(DN_ALPHA * x + g2 * moe, ln2_g[l], ln2_b[l])
        if need_ctx:
            ctx = layer_norm(DN_ALPHA * ctx + g1c * oc, ln1_g[l], ln1_b[l])
            moe_c = expert_choice(ctx * (1.0 + sc2c) + sh2c, w_router[l], w_gate[l], w_up[l], w_down[l])
            ctx = layer_norm(DN_ALPHA * ctx + g2c * moe_c, ln2_g[l], ln2_b[l])
    return x
```

```python
import functools
import math

import jax
import jax.numpy as jnp
from jax import lax
from jax.experimental import pallas as pl
from jax.experimental.pallas import tpu as pltpu

F32 = jnp.float32
BF16 = jnp.bfloat16
I32 = jnp.int32

D_MODEL = 1024
DEPTH = 2
GRID_W = 64
HEAD_DIM = 64
N_FREQ = HEAD_DIM // 4
ROPE_THETA = 10000.0
A_HEADS = 8
A_KV_HEADS = 2
LRU_WIDTH = 512
LRU_BLOCKS = 8
CONV_WIDTH = 4
LRU_C = 8.0
C_HEADS = 4
N_BRANCH = 3
BRANCH_W = 512
N_EXPERTS = 16
EC_FACTOR = 2
DN_ALPHA = (2 * DEPTH) ** 0.25
LN_EPS = 1e-5
RMS_EPS = 1e-6

LANES = 128
SUBLANES = 8
BF16_ROWS = 16
MXU_DIM = 256
VMEM_LIMIT = 56 * 1024 * 1024

LOG2E = 1.4426950408889634
QSCALE = (HEAD_DIM ** -0.5) * LOG2E

P1_GM = 0
P1_XB = N_BRANCH * D_MODEL
P1_W = P1_XB + LRU_WIDTH
P2_QA, P2_GB, P2_QC, P2_KC, P2_VC, P2_KA, P2_VA = 0, 512, 1024, 1536, 2048, 2560, 2688
P2_W = 2816


def _cparams(sem, vmem=VMEM_LIMIT):
    return pltpu.CompilerParams(dimension_semantics=sem, vmem_limit_bytes=vmem)


def _lane_iota(shape):
    return lax.broadcasted_iota(I32, shape, len(shape) - 1)


def _layer_norm(z, g, b):
    mu = jnp.mean(z, axis=-1, keepdims=True)
    zc = z - mu
    var = jnp.mean(zc * zc, axis=-1, keepdims=True)
    return (zc * lax.rsqrt(var + LN_EPS)) * g + b


def _rms_heads64(x, g128):
    lo = _lane_iota(x.shape) < HEAD_DIM
    x2 = x * x
    s_lo = jnp.sum(jnp.where(lo, x2, 0.0), axis=-1, keepdims=True)
    s_hi = jnp.sum(jnp.where(lo, 0.0, x2), axis=-1, keepdims=True)
    inv = jnp.where(lo, lax.rsqrt(s_lo * (1.0 / HEAD_DIM) + RMS_EPS),
                    lax.rsqrt(s_hi * (1.0 / HEAD_DIM) + RMS_EPS))
    return (x * inv) * g128


def _rope128(x, c, s):
    first = (_lane_iota(x.shape) & (2 * N_FREQ - 1)) < N_FREQ
    partner = jnp.where(first, pltpu.roll(x, LANES - N_FREQ, 1), pltpu.roll(x, N_FREQ, 1))
    return x * c + partner * s


def _ada_kernel(c_ref, w_ref, b_ref, o_ref):
    c = c_ref[...]
    s = c * jax.nn.sigmoid(c)
    o_ref[0] = jnp.dot(s.astype(BF16), w_ref[0].astype(BF16), preferred_element_type=F32) + b_ref[0]


def _ada(cc, w_ada, b_ada):
    nl, d, n6 = w_ada.shape
    tn = 1536
    return pl.pallas_call(
        _ada_kernel,
        grid=(nl, n6 // tn),
        in_specs=[pl.BlockSpec((SUBLANES, d), lambda l, j: (0, 0)),
                  pl.BlockSpec((1, d, tn), lambda l, j: (l, 0, j)),
                  pl.BlockSpec((1, 1, tn), lambda l, j: (l, 0, j))],
        out_specs=pl.BlockSpec((1, SUBLANES, tn), lambda l, j: (l, 0, j)),
        out_shape=jax.ShapeDtypeStruct((nl, SUBLANES, n6), F32),
        compiler_params=_cparams(("arbitrary", "arbitrary")),
        name="ada",
    )(cc, w_ada, b_ada.reshape(nl, 1, n6))


def _mod_matmul_kernel(x_ref, sc_ref, sh_ref, w_ref, o_ref):
    h = x_ref[0] * (1.0 + sc_ref[0]) + sh_ref[0]
    o_ref[0] = jnp.dot(h.astype(BF16), w_ref[...], preferred_element_type=F32).astype(o_ref.dtype)


def _mod_matmul(x, sc, sh, w, out_dtype, tn, name):
    g, r, k = x.shape
    n = w.shape[1]
    tm = min(512, r)
    return pl.pallas_call(
        _mod_matmul_kernel,
        grid=(g, r // tm, n // tn),
        in_specs=[pl.BlockSpec((1, tm, k), lambda a, i, j: (a, i, 0)),
                  pl.BlockSpec((1, 1, k), lambda a, i, j: (a, 0, 0)),
                  pl.BlockSpec((1, 1, k), lambda a, i, j: (a, 0, 0)),
                  pl.BlockSpec((k, tn), lambda a, i, j: (0, j))],
        out_specs=pl.BlockSpec((1, tm, tn), lambda a, i, j: (a, i, j)),
        out_shape=jax.ShapeDtypeStruct((g, r, n), out_dtype),
        compiler_params=_cparams(("arbitrary", "arbitrary", "arbitrary")),
        name=name,
    )(x, sc, sh, w)


def _prep_kv_kernel(kc_l, vc_l, ka_l, va_l, kc_c, vc_c, ka_c, va_c, cos_ref, sin_ref, gk_ref,
                    ka_o, va_o, kc_o, vc_o):
    i = pl.program_id(1)

    @pl.when(i == 0)
    def _():
        ka_o[0] = _rms_heads64(ka_c[0].astype(F32), gk_ref[...]).astype(BF16)
        va_o[0] = va_c[0]
        kc_o[0] = kc_c[0]
        vc_o[0] = vc_c[0]

    @pl.when(i > 0)
    def _():
        c, s = cos_ref[...], sin_ref[...]
        ka = _rms_heads64(ka_l[0].astype(F32), gk_ref[...])
        ka_o[0] = _rope128(ka, c, s).astype(BF16)
        va_o[0] = va_l[0]
        for j in range(4):
            sl = slice(LANES * j, LANES * (j + 1))
            kc_o[0, :, sl] = _rope128(kc_l[0, :, sl].astype(F32), c, s).astype(BF16)
        vc_o[0] = vc_l[0]


def _prep_kv(pb_l, pb_c, cos_t, sin_t, gk128):
    b, s, _ = pb_l.shape
    lc = pb_c.shape[1]
    tr = lc
    nb = (lc + s) // tr
    lat = lambda col: (lambda bb, i: (bb, jnp.maximum(i - 1, 0), col))
    ctx = lambda col: (lambda bb, i: (bb, 0, col))
    wide, nar = (1, tr, 512), (1, tr, LANES)
    outw = lambda w: pl.BlockSpec((1, tr, w), lambda bb, i: (bb, i, 0))
    return pl.pallas_call(
        _prep_kv_kernel,
        grid=(b, nb),
        in_specs=[pl.BlockSpec(wide, lat(P2_KC // 512)), pl.BlockSpec(wide, lat(P2_VC // 512)),
                  pl.BlockSpec(nar, lat(P2_KA // LANES)), pl.BlockSpec(nar, lat(P2_VA // LANES)),
                  pl.BlockSpec(wide, ctx(P2_KC // 512)), pl.BlockSpec(wide, ctx(P2_VC // 512)),
                  pl.BlockSpec(nar, ctx(P2_KA // LANES)), pl.BlockSpec(nar, ctx(P2_VA // LANES)),
                  pl.BlockSpec((tr, LANES), lambda bb, i: (jnp.maximum(i - 1, 0), 0)),
                  pl.BlockSpec((tr, LANES), lambda bb, i: (jnp.maximum(i - 1, 0), 0)),
                  pl.BlockSpec((1, LANES), lambda bb, i: (0, 0))],
        out_specs=[outw(LANES), outw(LANES), outw(512), outw(512)],
        out_shape=[jax.ShapeDtypeStruct((b, lc + s, LANES), BF16),
                   jax.ShapeDtypeStruct((b, lc + s, LANES), BF16),
                   jax.ShapeDtypeStruct((b, lc + s, 512), BF16),
                   jax.ShapeDtypeStruct((b, lc + s, 512), BF16)],
        compiler_params=_cparams(("arbitrary", "arbitrary")),
        name="prep_kv",
    )(pb_l, pb_l, pb_l, pb_l, pb_c, pb_c, pb_c, pb_c, cos_t, sin_t, gk128)


def _pick_tk(kv):
    for tk in (768, 512, 384, 256, 128):
        if kv % tk == 0:
            return tk
    raise ValueError(f"key count {kv} must be a multiple of {LANES}")


def _flash_loop(qs_ref, k_ref, v_ref, m_ref, l_ref, acc_ref, tk, nkv):
    m_ref[...] = jnp.full(m_ref.shape, -jnp.inf, F32)
    l_ref[...] = jnp.zeros(l_ref.shape, F32)
    acc_ref[...] = jnp.zeros(acc_ref.shape, F32)

    def body(c, carry):
        off = pl.multiple_of(c * tk, tk)
        k = k_ref[0, pl.ds(off, tk), :]
        v = v_ref[0, pl.ds(off, tk), :]
        s = lax.dot_general(qs_ref[...], k, (((1,), (1,)), ((), ())), preferred_element_type=F32)
        m_prev = m_ref[...]
        m_new = jnp.maximum(m_prev, jnp.max(s, axis=1, keepdims=True))
        alpha = jnp.exp2(m_prev - m_new)
        p = jnp.exp2(s - jnp.concatenate([m_new] * (tk // LANES), axis=1))
        l_ref[...] = alpha * l_ref[...] + jnp.sum(p, axis=1, keepdims=True)
        acc_ref[...] = alpha * acc_ref[...] + jnp.dot(p.astype(BF16), v, preferred_element_type=F32)
        m_ref[...] = m_new
        return carry

    lax.fori_loop(0, nkv, body, 0)


def _attn_a_kernel(q_ref, k_ref, v_ref, cos_ref, sin_ref, gq_ref, o_ref, qs_ref, m_ref, l_ref, acc_ref,
                   *, tq, tk, nkv, use_rope):
    lo = _lane_iota((tq, LANES)) < HEAD_DIM
    for j in range(4):
        x = _rms_heads64(q_ref[0, :, LANES * j:LANES * (j + 1)].astype(F32), gq_ref[...])
        if use_rope:
            x = _rope128(x, cos_ref[...], sin_ref[...])
        x = x * QSCALE
        xr = pltpu.roll(x, HEAD_DIM, 1)
        if j < 2:
            even, odd = jnp.where(lo, x, 0.0), jnp.where(lo, xr, 0.0)
        else:
            even, odd = jnp.where(lo, 0.0, xr), jnp.where(lo, 0.0, x)
        qs_ref[(2 * j) * tq:(2 * j + 1) * tq, :] = even.astype(BF16)
        qs_ref[(2 * j + 1) * tq:(2 * j + 2) * tq, :] = odd.astype(BF16)

    _flash_loop(qs_ref, k_ref, v_ref, m_ref, l_ref, acc_ref, tk, nkv)

    o = acc_ref[...] * (1.0 / l_ref[...])
    for j in range(4):
        even = o[(2 * j) * tq:(2 * j + 1) * tq]
        odd = o[(2 * j + 1) * tq:(2 * j + 2) * tq]
        if j < 2:
            col = jnp.where(lo, even, pltpu.roll(odd, HEAD_DIM, 1))
        else:
            col = jnp.where(lo, pltpu.roll(even, HEAD_DIM, 1), odd)
        o_ref[0, :, LANES * j:LANES * (j + 1)] = col.astype(BF16)


def _attn_a(pb_q, k_all, v_all, cos_t, sin_t, gq128, kv_rows, use_rope):
    b, s, _ = pb_q.shape
    tq = min(128, s)
    tk = _pick_tk(kv_rows)
    rows = A_HEADS * tq
    kern = functools.partial(_attn_a_kernel, tq=tq, tk=tk, nkv=kv_rows // tk, use_rope=use_rope)
    return pl.pallas_call(
        kern,
        grid=(b, s // tq),
        in_specs=[pl.BlockSpec((1, tq, 512), lambda bb, i: (bb, i, P2_QA // 512)),
                  pl.BlockSpec((1, kv_rows, LANES), lambda bb, i: (bb, 0, 0)),
                  pl.BlockSpec((1, kv_rows, LANES), lambda bb, i: (bb, 0, 0)),
                  pl.BlockSpec((tq, LANES), lambda bb, i: (i, 0)),
                  pl.BlockSpec((tq, LANES), lambda bb, i: (i, 0)),
                  pl.BlockSpec((1, LANES), lambda bb, i: (0, 0))],
        out_specs=pl.BlockSpec((1, tq, 512), lambda bb, i: (bb, i, 0)),
        out_shape=jax.ShapeDtypeStruct((b, s, 512), BF16),
        scratch_shapes=[pltpu.VMEM((rows, LANES), BF16), pltpu.VMEM((rows, LANES), F32),
                        pltpu.VMEM((rows, LANES), F32), pltpu.VMEM((rows, LANES), F32)],
        compiler_params=_cparams(("arbitrary", "arbitrary")),
        name="attn_gqa",
    )(pb_q, k_all, v_all, cos_t, sin_t, gq128)


def _attn_c_kernel(q_ref, k_ref, v_ref, cos_ref, sin_ref, dl_ref, sub_ref, o_ref, qs_ref, m_ref, l_ref,
                   acc_ref, *, tq, tk, nkv, use_rope, lam_init):
    lo = _lane_iota((tq, LANES)) < HEAD_DIM
    x = q_ref[0].astype(F32)
    if use_rope:
        x = _rope128(x, cos_ref[...], sin_ref[...])
    x = x * QSCALE
    qs_ref[0:tq, :] = jnp.where(lo, x, 0.0).astype(BF16)
    qs_ref[tq:2 * tq, :] = jnp.where(lo, 0.0, x).astype(BF16)

    _flash_loop(qs_ref, k_ref, v_ref, m_ref, l_ref, acc_ref, tk, nkv)

    dl = dl_ref[...]
    lam = (jnp.exp(jnp.sum(dl[0:1] * dl[1:2], axis=1, keepdims=True))
           - jnp.exp(jnp.sum(dl[2:3] * dl[3:4], axis=1, keepdims=True)) + lam_init)
    o = acc_ref[...] * (1.0 / l_ref[...])
    y = o[0:tq] - lam * o[tq:2 * tq]
    ms = jnp.mean(y * y, axis=1, keepdims=True)
    y = ((y * lax.rsqrt(ms + RMS_EPS)) * sub_ref[...]) * (1.0 - lam_init)
    o_ref[0] = y.astype(BF16)


def _attn_c(pb_q, k_all, v_all, cos_t, sin_t, dlam, subln, kv_rows, use_rope, lam_init):
    b, s, _ = pb_q.shape
    tq = min(256, s)
    tk = _pick_tk(kv_rows)
    rows = 2 * tq
    kern = functools.partial(_attn_c_kernel, tq=tq, tk=tk, nkv=kv_rows // tk, use_rope=use_rope,
                             lam_init=lam_init)
    return pl.pallas_call(
        kern,
        grid=(b, C_HEADS, s // tq),
        in_specs=[pl.BlockSpec((1, tq, LANES), lambda bb, h, i: (bb, i, P2_QC // LANES + h)),
                  pl.BlockSpec((1, kv_rows, LANES), lambda bb, h, i: (bb, 0, h)),
                  pl.BlockSpec((1, kv_rows, LANES), lambda bb, h, i: (bb, 0, h)),
                  pl.BlockSpec((tq, LANES), lambda bb, h, i: (i, 0)),
                  pl.BlockSpec((tq, LANES), lambda bb, h, i: (i, 0)),
                  pl.BlockSpec((4, HEAD_DIM), lambda bb, h, i: (0, 0)),
                  pl.BlockSpec((1, LANES), lambda bb, h, i: (0, 0))],
        out_specs=pl.BlockSpec((1, tq, LANES), lambda bb, h, i: (bb, i, h)),
        out_shape=jax.ShapeDtypeStruct((b, s, 512), BF16),
        scratch_shapes=[pltpu.VMEM((rows, LANES), BF16), pltpu.VMEM((rows, LANES), F32),
                        pltpu.VMEM((rows, LANES), F32), pltpu.VMEM((rows, LANES), F32)],
        compiler_params=_cparams(("arbitrary", "arbitrary", "arbitrary")),
        name="attn_diff",
    )(pb_q, k_all, v_all, cos_t, sin_t, dlam, subln)


def _gelu_tanh(x):
    return 0.5 * x * (1.0 + jnp.tanh(math.sqrt(2.0 / math.pi) * (x + 0.044715 * (x * x * x))))


def _lru_kernel(*refs, nb, t, nchunks, reverse, final):
    (x_ref, xp_ref, xn_ref, h0_ref, cw_ref, cb_ref, wa_ref, ba_ref, wx_ref, bx_ref, lam_ref) = refs[:11]
    if final:
        gb_ref, hp_ref, y_ref, hl_ref, a_s, b_s, o_s, h_s = refs[11:]
    else:
        y_ref, hl_ref, a_s, b_s, o_s, h_s = refs[11:]
    c = pl.program_id(0)
    pos = (nchunks - 1 - c) if reverse else c

    @pl.when(c == 0)
    def _():
        h_s[...] = h0_ref[...]

    z = -lam_ref[...]
    softplus = jnp.maximum(z, 0.0) + jnp.log(1.0 + jnp.exp(-jnp.abs(z)))
    cw = cw_ref[...]
    for b in range(nb):
        x = x_ref[b]
        prev = jnp.where(pos == 0, 0.0, xp_ref[b])
        nxt = jnp.where(pos == nchunks - 1, 0.0, xn_ref[b])
        ext = jnp.concatenate([prev, x, nxt], axis=0)
        u = (cw[0:1] * ext[6:6 + t] + cw[1:2] * ext[7:7 + t] + cw[2:3] * x
             + cw[3:4] * ext[9:9 + t]) + cb_ref[...]
        ub = u.astype(BF16)
        r = jax.nn.sigmoid(jnp.dot(ub, wa_ref[...], preferred_element_type=F32) + ba_ref[...])
        gi = jax.nn.sigmoid(jnp.dot(ub, wx_ref[...], preferred_element_type=F32) + bx_ref[...])
        log_a = (-LRU_C * r) * softplus
        a = jnp.exp(log_a)
        mult = jnp.sqrt(-jnp.tanh(log_a) * (a * a + 1.0))
        a_s[b] = a
        b_s[b] = mult * (gi * u)

    ngroups = t // SUBLANES
    sub = lax.broadcasted_iota(I32, (SUBLANES, LRU_WIDTH), 0)
    order = range(SUBLANES - 1, -1, -1) if reverse else range(SUBLANES)

    def group(gidx, hs):
        g = (ngroups - 1 - gidx) if reverse else gidx
        base = pl.multiple_of(g * SUBLANES, SUBLANES)
        new = []
        for b in range(nb):
            ta = a_s[b, pl.ds(base, SUBLANES), :]
            tb = b_s[b, pl.ds(base, SUBLANES), :]
            h = hs[b]
            out = jnp.zeros((SUBLANES, LRU_WIDTH), F32)
            for r in order:
                h = (jnp.broadcast_to(ta[r:r + 1], h.shape) * h
                     + jnp.broadcast_to(tb[r:r + 1], h.shape))
                out = jnp.where(sub == r, h, out)
            o_s[b, pl.ds(base, SUBLANES), :] = out
            new.append(h)
        return tuple(new)

    hs = lax.fori_loop(0, ngroups, group, tuple(h_s[b] for b in range(nb)))
    for b in range(nb):
        h_s[b] = hs[b]
        if final:
            y = hp_ref[b] + o_s[b]
            y_ref[b] = (y * _gelu_tanh(gb_ref[b].astype(F32))).astype(y_ref.dtype)
        else:
            y_ref[b] = o_s[b]
    hl_ref[...] = h_s[...]


def _lru_pass(p1, h0, prm, d, reverse, gb_src=None, hprev=None):
    nb, length, _ = p1.shape
    t = min(256, length)
    nchunks = length // t
    final = gb_src is not None
    blk8 = t // SUBLANES
    nblk8 = length // SUBLANES
    xcol = P1_XB // LRU_WIDTH
    cpos = (lambda c: nchunks - 1 - c) if reverse else (lambda c: c)
    vec = lambda: pl.BlockSpec((1, LRU_WIDTH), lambda c: (0, 0))
    mat = lambda: pl.BlockSpec((LRU_WIDTH, LRU_WIDTH), lambda c: (0, 0))
    in_specs = [pl.BlockSpec((nb, t, LRU_WIDTH), lambda c: (0, cpos(c), xcol)),
                pl.BlockSpec((nb, SUBLANES, LRU_WIDTH),
                             lambda c: (0, jnp.maximum(cpos(c) * blk8 - 1, 0), xcol)),
                pl.BlockSpec((nb, SUBLANES, LRU_WIDTH),
                             lambda c: (0, jnp.minimum((cpos(c) + 1) * blk8, nblk8 - 1), xcol)),
                pl.BlockSpec((nb, SUBLANES, LRU_WIDTH), lambda c: (0, 0, 0)),
                pl.BlockSpec((CONV_WIDTH, LRU_WIDTH), lambda c: (0, 0)), vec(),
                mat(), vec(), mat(), vec(), vec()]
    args = [p1, p1, p1, h0, prm["conv_w"], prm["conv_b"], prm["wa"][d], prm["ba"][d],
            prm["wx"][d], prm["bx"][d], prm["lam"][d]]
    if final:
        in_specs += [pl.BlockSpec((nb, t, LRU_WIDTH), lambda c: (0, cpos(c), P2_GB // LRU_WIDTH)),
                     pl.BlockSpec((nb, t, LRU_WIDTH), lambda c: (0, cpos(c), 0))]
        args += [gb_src, hprev]
    kern = functools.partial(_lru_kernel, nb=nb, t=t, nchunks=nchunks, reverse=reverse, final=final)
    return pl.pallas_call(
        kern,
        grid=(nchunks,),
        in_specs=in_specs,
        out_specs=[pl.BlockSpec((nb, t, LRU_WIDTH), lambda c: (0, cpos(c), 0)),
                   pl.BlockSpec((nb, SUBLANES, LRU_WIDTH), lambda c: (0, 0, 0))],
        out_shape=[jax.ShapeDtypeStruct((nb, length, LRU_WIDTH), BF16 if final else F32),
                   jax.ShapeDtypeStruct((nb, SUBLANES, LRU_WIDTH), F32)],
        scratch_shapes=[pltpu.VMEM((nb, t, LRU_WIDTH), F32), pltpu.VMEM((nb, t, LRU_WIDTH), F32),
                        pltpu.VMEM((nb, t, LRU_WIDTH), F32), pltpu.VMEM((nb, SUBLANES, LRU_WIDTH), F32)],
        compiler_params=_cparams(("arbitrary",)),
        name="lru_bwd" if reverse else "lru_fwd",
    )(*args)


def _merge_kernel(ya_ref, yb_ref, yc_ref, gm_ref, x_ref, g1_ref, sc2_ref, sh2_ref, lg_ref, lb_ref,
                  wb_ref, wo_ref, wr_ref, x1_ref, h2_ref, aff_ref):
    acc = None
    for n, br in enumerate((ya_ref, yb_ref, yc_ref)):
        p = jnp.dot(br[0], wb_ref[n], preferred_element_type=F32)
        gate = jax.nn.sigmoid(gm_ref[0, :, n * D_MODEL:(n + 1) * D_MODEL])
        acc = gate * p if acc is None else acc + gate * p
    o = jnp.dot(acc.astype(BF16), wo_ref[...], preferred_element_type=F32)
    x1 = _layer_norm(DN_ALPHA * x_ref[0] + g1_ref[0] * o, lg_ref[...], lb_ref[...])
    x1_ref[0] = x1
    h2 = x1 * (1.0 + sc2_ref[0]) + sh2_ref[0]
    h2_ref[0] = h2
    logits = jnp.dot(h2.astype(BF16), wr_ref[...], preferred_element_type=F32)
    logits = jnp.where(_lane_iota(logits.shape) < N_EXPERTS, logits, -jnp.inf)
    e = jnp.exp(logits - jnp.max(logits, axis=-1, keepdims=True))
    aff_ref[0] = e / jnp.sum(e, axis=-1, keepdims=True)


def _merge(ya, yb, yc, p1, x, g1, sc2, sh2, lg, lb, wb, wo, wr):
    b, s, d = x.shape
    tm = min(512, s)
    row = lambda w, col=0: pl.BlockSpec((1, tm, w), lambda bb, i: (bb, i, col))
    per_b = lambda: pl.BlockSpec((1, 1, d), lambda bb, i: (bb, 0, 0))
    full = lambda shape: pl.BlockSpec(shape, lambda bb, i: (0,) * len(shape))
    return pl.pallas_call(
        _merge_kernel,
        grid=(b, s // tm),
        in_specs=[row(512), row(512), row(512), row(N_BRANCH * d, P1_GM // (N_BRANCH * d)), row(d),
                  per_b(), per_b(), per_b(), full((1, d)), full((1, d)),
                  full((N_BRANCH, BRANCH_W, d)), full((d, d)), full((d, LANES))],
        out_specs=[row(d), row(d), row(LANES)],
        out_shape=[jax.ShapeDtypeStruct((b, s, d), F32), jax.ShapeDtypeStruct((b, s, d), F32),
                   jax.ShapeDtypeStruct((b, s, LANES), F32)],
        compiler_params=_cparams(("arbitrary", "arbitrary")),
        name="merge_ln1_router",
    )(ya, yb, yc, p1, x, g1, sc2, sh2, lg, lb, wb, wo, wr)


def _route_kernel(aff_ref, pos_ref, base_ref, x_s, c_s, *, n, cap, blk):
    vi = pltpu.bitcast(aff_ref[0], I32)
    kf = float(cap)

    def bit_body(i, thr):
        cand = thr | lax.shift_left(jnp.int32(1), 30 - i)
        cnt = jnp.sum(jnp.where(vi >= cand, 1.0, 0.0), axis=0, keepdims=True)
        return jnp.where(cnt >= kf, cand, thr)

    thr = lax.fori_loop(0, 31, bit_body, jnp.zeros((1, LANES), I32))
    gt = vi > thr
    eq = vi == thr
    need = kf - jnp.sum(jnp.where(gt, 1.0, 0.0), axis=0, keepdims=True)

    nblk = n // blk
    tri = (lax.broadcasted_iota(I32, (blk, blk), 0) > lax.broadcasted_iota(I32, (blk, blk), 1))
    tri = jnp.where(tri, 1.0, 0.0).astype(BF16)

    def excl_cumsum(write_base):
        carry = jnp.zeros((1, LANES), F32)
        for j in range(nblk):
            xb = x_s[j * blk:(j + 1) * blk, :]
            c_s[j * blk:(j + 1) * blk, :] = (
                jnp.dot(tri, xb.astype(BF16), preferred_element_type=F32) + carry)
            if write_base:
                base_ref[0, j:j + 1, :] = carry.astype(I32)
            carry = carry + jnp.sum(xb, axis=0, keepdims=True)

    x_s[...] = jnp.where(eq, 1.0, 0.0)
    excl_cumsum(False)
    take_eq = jnp.where(c_s[...] < need, 1.0, 0.0)
    sel = jnp.where(gt, 1.0, jnp.where(eq, take_eq, 0.0))
    sel = jnp.where(_lane_iota(sel.shape) < N_EXPERTS, sel, 0.0)
    x_s[...] = sel
    excl_cumsum(True)
    pos_ref[0] = jnp.where(x_s[...] > 0.5, c_s[...].astype(I32), -1)


def _route(aff, cap, blk):
    b, n, _ = aff.shape
    kern = functools.partial(_route_kernel, n=n, cap=cap, blk=blk)
    return pl.pallas_call(
        kern,
        grid=(b,),
        in_specs=[pl.BlockSpec((1, n, LANES), lambda bb: (bb, 0, 0))],
        out_specs=[pl.BlockSpec((1, n, LANES), lambda bb: (bb, 0, 0)),
                   pl.BlockSpec((1, n // blk, LANES), lambda bb: (bb, 0, 0))],
        out_shape=[jax.ShapeDtypeStruct((b, n, LANES), I32),
                   jax.ShapeDtypeStruct((b, n // blk, LANES), I32)],
        scratch_shapes=[pltpu.VMEM((n, LANES), F32), pltpu.VMEM((n, LANES), F32)],
        compiler_params=_cparams(("arbitrary",)),
        name="route",
    )(aff)


def _expert_kernel(idx_ref, h2_hbm, wg_ref, wu_ref, wd_ref, y_ref, xbuf, sem, *, nb, ne, cap, tsub):
    e = pl.program_id(0)
    b = pl.program_id(1)
    step = e * nb + b
    slot = step % 2

    def gather(st, sl):
        bb = st % nb
        off = (bb * ne + st // nb) * cap

        def body(r, carry):
            tok = idx_ref[off + r]
            pltpu.make_async_copy(h2_hbm.at[bb, pl.ds(tok, 1), :], xbuf.at[sl, pl.ds(r, 1), :],
                                  sem.at[sl]).start()
            return carry

        lax.fori_loop(0, cap, body, 0, unroll=8)

    @pl.when(step == 0)
    def _():
        gather(step, slot)

    @pl.when(step + 1 < nb * ne)
    def _():
        gather(step + 1, 1 - slot)

    pltpu.make_async_copy(xbuf.at[slot], xbuf.at[slot], sem.at[slot]).wait()

    for s in range(cap // tsub):
        rows = slice(s * tsub, (s + 1) * tsub)
        x = xbuf[slot, rows, :].astype(BF16)
        g = jnp.dot(x, wg_ref[0], preferred_element_type=F32)
        u = jnp.dot(x, wu_ref[0], preferred_element_type=F32)
        act = ((g * jax.nn.sigmoid(g)) * u).astype(BF16)
        y_ref[0, 0, rows, :] = jnp.dot(act, wd_ref[0], preferred_element_type=F32).astype(y_ref.dtype)


def _experts(idx_flat, h2, wg, wu, wd, cap):
    nb, n, d = h2.shape
    ne, _, fp = wg.shape
    tsub = min(256, cap)
    kern = functools.partial(_expert_kernel, nb=nb, ne=ne, cap=cap, tsub=tsub)
    grid_spec = pltpu.PrefetchScalarGridSpec(
        num_scalar_prefetch=1,
        grid=(ne, nb),
        in_specs=[pl.BlockSpec(memory_space=pl.ANY),
                  pl.BlockSpec((1, d, fp), lambda e, b, idx: (e, 0, 0)),
                  pl.BlockSpec((1, d, fp), lambda e, b, idx: (e, 0, 0)),
                  pl.BlockSpec((1, fp, d), lambda e, b, idx: (e, 0, 0))],
        out_specs=pl.BlockSpec((1, 1, cap, d), lambda e, b, idx: (b, e, 0, 0)),
        scratch_shapes=[pltpu.VMEM((2, cap, d), F32), pltpu.SemaphoreType.DMA((2,))],
    )
    return pl.pallas_call(
        kern,
        grid_spec=grid_spec,
        out_shape=jax.ShapeDtypeStruct((nb, ne, cap, d), BF16),
        compiler_params=_cparams(("arbitrary", "arbitrary"), 60 * 1024 * 1024),
        name="expert_ffn",
    )(idx_flat, h2, wg, wu, wd)


def _combine_kernel(base_ref, x1_ref, pos_ref, aff_ref, g2_ref, lg_ref, lb_ref, y_hbm, o_ref, ybuf, sem,
                    *, nb, nt, ne, cap, tt, win):
    b = pl.program_id(0)
    i = pl.program_id(1)
    step = b * nt + i
    slot = step % 2

    def win_start(st, e):
        base = base_ref[st * ne + e]
        al = lax.shift_left(lax.shift_right_logical(base, 4), 4)
        return pl.multiple_of(jnp.minimum(al, cap - win), BF16_ROWS)

    def fetch(st, sl):
        bb = st // nt
        for e in range(ne):
            pltpu.make_async_copy(y_hbm.at[bb, e, pl.ds(win_start(st, e), win), :], ybuf.at[sl, e],
                                  sem.at[sl]).start()

    @pl.when(step == 0)
    def _():
        fetch(step, slot)

    @pl.when(step + 1 < nb * nt)
    def _():
        fetch(step + 1, 1 - slot)

    for e in range(ne):
        pltpu.make_async_copy(y_hbm.at[0, e, pl.ds(0, win), :], ybuf.at[slot, e], sem.at[slot]).wait()

    pos = pos_ref[0]
    aff = aff_ref[0]
    col = _lane_iota((tt, win))
    acc = jnp.zeros((tt, D_MODEL), F32)
    for e in range(ne):
        rel = pos[:, e:e + 1] - win_start(step, e)
        onehot = jnp.where(col == rel, 1.0, 0.0).astype(BF16)
        acc = acc + aff[:, e:e + 1] * jnp.dot(onehot, ybuf[slot, e], preferred_element_type=F32)
    o_ref[0] = _layer_norm(DN_ALPHA * x1_ref[0] + g2_ref[0] * acc, lg_ref[...], lb_ref[...])


def _combine(base_flat, x1, pos, aff, g2, lg, lb, y, tt):
    nb, n, d = x1.shape
    ne, cap = y.shape[1], y.shape[2]
    nt = n // tt
    win = min(tt + BF16_ROWS, cap)
    kern = functools.partial(_combine_kernel, nb=nb, nt=nt, ne=ne, cap=cap, tt=tt, win=win)
    row = lambda w: pl.BlockSpec((1, tt, w), lambda b, i, base: (b, i, 0))
    grid_spec = pltpu.PrefetchScalarGridSpec(
        num_scalar_prefetch=1,
        grid=(nb, nt),
        in_specs=[row(d), row(LANES), row(LANES),
                  pl.BlockSpec((1, 1, d), lambda b, i, base: (b, 0, 0)),
                  pl.BlockSpec((1, d), lambda b, i, base: (0, 0)),
                  pl.BlockSpec((1, d), lambda b, i, base: (0, 0)),
                  pl.BlockSpec(memory_space=pl.ANY)],
        out_specs=row(d),
        scratch_shapes=[pltpu.VMEM((2, ne, win, d), BF16), pltpu.SemaphoreType.DMA((2,))],
    )
    return pl.pallas_call(
        kern,
        grid_spec=grid_spec,
        out_shape=jax.ShapeDtypeStruct((nb, n, d), F32),
        compiler_params=_cparams(("arbitrary", "arbitrary")),
        name="combine_ln2",
    )(base_flat, x1, pos, aff, g2, lg, lb, y)


def _moe(x1, h2, aff, g2, lg, lb, wg, wu, wd):
    nb, n, d = x1.shape
    cap = EC_FACTOR * n // N_EXPERTS
    tt = min(128, n)
    pos, base = _route(aff, cap, tt)
    p = pos[:, :, :N_EXPERTS]
    p = jnp.where(p < 0, cap, p)
    bi = jnp.arange(nb, dtype=I32)[:, None, None]
    ei = jnp.arange(N_EXPERTS, dtype=I32)[None, None, :]
    ti = jnp.broadcast_to(jnp.arange(n, dtype=I32)[None, :, None], p.shape)
    idx = jnp.zeros((nb, N_EXPERTS, cap), I32).at[bi, ei, p].set(ti, mode="drop")
    y = _experts(idx.reshape(-1), h2, wg, wu, wd, cap)
    base_flat = base[:, :, :N_EXPERTS].reshape(-1)
    return _combine(base_flat, x1, pos, aff, g2, lg, lb, y, tt)


def _rope_tables(seq):
    rows = seq // GRID_W
    row = jnp.repeat(jnp.arange(rows, dtype=I32), GRID_W)
    colm = jnp.tile(jnp.arange(GRID_W, dtype=I32), rows)
    inv = ROPE_THETA ** (-jnp.arange(N_FREQ, dtype=F32) / N_FREQ)
    ang = jnp.stack([row, colm], -1).astype(F32)[..., None] * inv
    cos, sin = jnp.cos(ang), jnp.sin(ang)
    c64 = jnp.concatenate([cos[:, 0], cos[:, 0], cos[:, 1], cos[:, 1]], axis=-1)
    s64 = jnp.concatenate([-sin[:, 0], sin[:, 0], -sin[:, 1], sin[:, 1]], axis=-1)
    return jnp.tile(c64, (1, 2)), jnp.tile(s64, (1, 2))


def _block_diag(w):
    nblk, bw, _ = w.shape
    eye = jnp.eye(nblk, dtype=w.dtype)
    return (eye[:, None, :, None] * w[:, :, None, :]).reshape(nblk * bw, nblk * bw)


def kernel(x, c, ctx, c_ctx, w_ada, b_ada, w_in, a_q_norm, a_k_norm, lru_conv_w, lru_conv_b, lru_w_a,
           lru_b_a, lru_w_x, lru_b_x, lru_lambda, diff_lambda, diff_subln, w_branch, w_out, ln1_g, ln1_b,
           w_router, w_gate, w_up, w_down, ln2_g, ln2_b):
    nb, seq, d = x.shape
    lc = ctx.shape[1]
    depth = w_in.shape[0]
    assert d == D_MODEL and nb + 1 <= SUBLANES and seq % lc == 0 and lc % LANES == 0

    cos_t, sin_t = _rope_tables(seq)
    cc = jnp.zeros((SUBLANES, d), F32).at[:nb].set(c).at[nb].set(c_ctx)
    mod_all = _ada(cc, w_ada, b_ada)

    fexp = w_gate.shape[-1]
    fpad = -(-fexp // MXU_DIM) * MXU_DIM
    zeros8 = jnp.zeros((nb, SUBLANES, LRU_WIDTH), F32)

    for l in range(depth):
        need_ctx = l < depth - 1
        lam_init = 0.8 - 0.6 * math.exp(-0.3 * l)
        mod = mod_all[l]
        m6 = mod[:nb].reshape(nb, 6, 1, d)
        sh1, sc1, g1, sh2, sc2, g2 = (m6[:, k] for k in range(6))
        mc6 = jnp.broadcast_to(mod[nb].reshape(1, 6, 1, d), (nb, 6, 1, d))
        sh1c, sc1c, g1c, sh2c, sc2c, g2c = (mc6[:, k] for k in range(6))

        wi = w_in[l]
        w1 = jnp.concatenate([wi[:, 3328:6400], wi[:, 768:1280]], axis=1).astype(BF16)
        w2 = jnp.concatenate([wi[:, 0:512], wi[:, 1280:1792], wi[:, 1792:2304], wi[:, 2304:2816],
                              wi[:, 2816:3328], wi[:, 512:640], wi[:, 640:768]], axis=1).astype(BF16)
        gq128 = jnp.tile(a_q_norm[l], 2).reshape(1, LANES)
        gk128 = jnp.tile(a_k_norm[l], 2).reshape(1, LANES)
        lru = {
            "conv_w": lru_conv_w[l], "conv_b": lru_conv_b[l].reshape(1, -1),
            "wa": [_block_diag(lru_w_a[l, k]).astype(BF16) for k in range(2)],
            "wx": [_block_diag(lru_w_x[l, k]).astype(BF16) for k in range(2)],
            "ba": [lru_b_a[l, k].reshape(1, -1) for k in range(2)],
            "bx": [lru_b_x[l, k].reshape(1, -1) for k in range(2)],
            "lam": [lru_lambda[l, k].reshape(1, -1) for k in range(2)],
        }
        subln = diff_subln[l].reshape(1, LANES)
        wb = w_branch[l].astype(BF16)
        wo = w_out[l].astype(BF16)
        wr = jnp.zeros((d, LANES), F32).at[:, :N_EXPERTS].set(w_router[l]).astype(BF16)
        padf = ((0, 0), (0, 0), (0, fpad - fexp))
        wg = jnp.pad(w_gate[l], padf).astype(BF16)
        wu = jnp.pad(w_up[l], padf).astype(BF16)
        wd = jnp.pad(w_down[l], ((0, 0), (0, fpad - fexp), (0, 0))).astype(BF16)
        lg1, lb1 = ln1_g[l].reshape(1, d), ln1_b[l].reshape(1, d)
        lg2, lb2 = ln2_g[l].reshape(1, d), ln2_b[l].reshape(1, d)

        p1 = _mod_matmul(x, sc1, sh1, w1, F32, P1_W // 2, "inproj_f32")
        p2 = _mod_matmul(x, sc1, sh1, w2, BF16, P2_W, "inproj_bf16")
        p1c = _mod_matmul(ctx, sc1c, sh1c, w1, F32, P1_W // 2, "inproj_f32")
        p2c = _mod_matmul(ctx, sc1c, sh1c, w2, BF16, P2_W, "inproj_bf16")

        ka, va, kc, vc = _prep_kv(p2, p2c, cos_t, sin_t, gk128)
        kv = lc + seq
        ya = _attn_a(p2, ka, va, cos_t, sin_t, gq128, kv, True)
        yc = _attn_c(p2, kc, vc, cos_t, sin_t, diff_lambda[l], subln, kv, True, lam_init)

        hf_c, hl_f = _lru_pass(p1c, zeros8, lru, 0, False)
        yb_c, hl_b = _lru_pass(p1c, zeros8, lru, 1, True, gb_src=p2c, hprev=hf_c)
        hf, _ = _lru_pass(p1, hl_f, lru, 0, False)
        yb, _ = _lru_pass(p1, hl_b, lru, 1, True, gb_src=p2, hprev=hf)

        x1, h2, aff = _merge(ya, yb, yc, p1, x, g1, sc2, sh2, lg1, lb1, wb, wo, wr)
        x_next = _moe(x1, h2, aff, g2, lg2, lb2, wg, wu, wd)

        if need_ctx:
            ya_c = _attn_a(p2c, ka, va, cos_t, sin_t, gq128, lc, False)
            yc_c = _attn_c(p2c, kc, vc, cos_t, sin_t, diff_lambda[l], subln, lc, False, lam_init)
            c1, hc2, affc = _merge(ya_c, yb_c, yc_c, p1c, ctx, g1c, sc2c, sh2c, lg1, lb1, wb, wo, wr)
            ctx = _moe(c1, hc2, affc, g2c, lg2, lb2, wg, wu, wd)
        x = x_next
    return x
```

```python
import functools
import math

import jax
import jax.numpy as jnp
from jax import lax
from jax.experimental import pallas as pl
from jax.experimental.pallas import tpu as pltpu

F32 = jnp.float32
BF16 = jnp.bfloat16
I32 = jnp.int32

D_MODEL = 1024
DEPTH = 2
GRID_W = 64
HEAD_DIM = 64
N_FREQ = HEAD_DIM // 4
ROPE_THETA = 10000.0
A_HEADS = 8
A_KV_HEADS = 2
LRU_WIDTH = 512
LRU_BLOCKS = 8
CONV_WIDTH = 4
LRU_C = 8.0
C_HEADS = 4
N_BRANCH = 3
BRANCH_W = 512
N_EXPERTS = 16
EC_FACTOR = 2
DN_ALPHA = (2 * DEPTH) ** 0.25
LN_EPS = 1e-5
RMS_EPS = 1e-6

LANES = 128
SUBLANES = 8
BF16_ROWS = 16
MXU_DIM = 256
QSTRIP = 2 * MXU_DIM
VMEM_LIMIT = 56 * 1024 * 1024

LOG2E = 1.4426950408889634
QSCALE = (HEAD_DIM ** -0.5) * LOG2E

P1_GM = 0
P1_XB = N_BRANCH * D_MODEL
P1_W = P1_XB + LRU_WIDTH
P2_QA, P2_GB, P2_QC, P2_KC, P2_VC, P2_KA, P2_VA = 0, 512, 1024, 1536, 2048, 2560, 2688
P2_W = 2816


def _cparams(sem, vmem=VMEM_LIMIT):
    return pltpu.CompilerParams(dimension_semantics=sem, vmem_limit_bytes=vmem)


def _lane_iota(shape):
    return lax.broadcasted_iota(I32, shape, len(shape) - 1)


def _layer_norm(z, g, b):
    mu = jnp.mean(z, axis=-1, keepdims=True)
    zc = z - mu
    var = jnp.mean(zc * zc, axis=-1, keepdims=True)
    return (zc * lax.rsqrt(var + LN_EPS)) * g + b


def _rms_heads64(x, g128):
    lo = _lane_iota(x.shape) < HEAD_DIM
    x2 = x * x
    s_lo = jnp.sum(jnp.where(lo, x2, 0.0), axis=-1, keepdims=True)
    s_hi = jnp.sum(jnp.where(lo, 0.0, x2), axis=-1, keepdims=True)
    inv = jnp.where(lo, lax.rsqrt(s_lo * (1.0 / HEAD_DIM) + RMS_EPS),
                    lax.rsqrt(s_hi * (1.0 / HEAD_DIM) + RMS_EPS))
    return (x * inv) * g128


def _rope128(x, c, s):
    first = (_lane_iota(x.shape) & (2 * N_FREQ - 1)) < N_FREQ
    partner = jnp.where(first, pltpu.roll(x, LANES - N_FREQ, 1), pltpu.roll(x, N_FREQ, 1))
    return x * c + partner * s


def _ada_kernel(c_ref, w_ref, b_ref, o_ref):
    c = c_ref[...]
    s = c * jax.nn.sigmoid(c)
    o_ref[0] = jnp.dot(s.astype(BF16), w_ref[0].astype(BF16), preferred_element_type=F32) + b_ref[0]


def _ada(cc, w_ada, b_ada):
    nl, d, n6 = w_ada.shape
    tn = 1536
    return pl.pallas_call(
        _ada_kernel,
        grid=(nl, n6 // tn),
        in_specs=[pl.BlockSpec((SUBLANES, d), lambda l, j: (0, 0)),
                  pl.BlockSpec((1, d, tn), lambda l, j: (l, 0, j)),
                  pl.BlockSpec((1, 1, tn), lambda l, j: (l, 0, j))],
        out_specs=pl.BlockSpec((1, SUBLANES, tn), lambda l, j: (l, 0, j)),
        out_shape=jax.ShapeDtypeStruct((nl, SUBLANES, n6), F32),
        compiler_params=_cparams(("arbitrary", "arbitrary")),
        name="ada",
    )(cc, w_ada, b_ada.reshape(nl, 1, n6))


def _mod_matmul_kernel(x_ref, sc_ref, sh_ref, w_ref, o_ref):
    h = x_ref[0] * (1.0 + sc_ref[0]) + sh_ref[0]
    o_ref[0] = jnp.dot(h.astype(BF16), w_ref[0], preferred_element_type=F32).astype(o_ref.dtype)


def _mod_matmul(x, sc, sh, w, l, out_dtype, tn, name):
    g, r, k = x.shape
    n = w.shape[2]
    tm = min(512, r)
    return pl.pallas_call(
        _mod_matmul_kernel,
        grid=(g, r // tm, n // tn),
        in_specs=[pl.BlockSpec((1, tm, k), lambda a, i, j: (a, i, 0)),
                  pl.BlockSpec((1, 1, k), lambda a, i, j: (a, 0, 0)),
                  pl.BlockSpec((1, 1, k), lambda a, i, j: (a, 0, 0)),
                  pl.BlockSpec((1, k, tn), lambda a, i, j: (l, 0, j))],
        out_specs=pl.BlockSpec((1, tm, tn), lambda a, i, j: (a, i, j)),
        out_shape=jax.ShapeDtypeStruct((g, r, n), out_dtype),
        compiler_params=_cparams(("arbitrary", "arbitrary", "arbitrary")),
        name=name,
    )(x, sc, sh, w)


def _prep_kv_kernel(kc_l, vc_l, ka_l, va_l, kc_c, vc_c, ka_c, va_c, cos_ref, sin_ref, gk_ref,
                    ka_o, vat_o, kc_o, vct_o):
    i = pl.program_id(1)

    def put_values(va, vc):
        vat_o[0, 0] = va.astype(F32).T.astype(BF16)
        for h in range(C_HEADS):
            vct_o[0, h, 0] = vc[:, LANES * h:LANES * (h + 1)].astype(F32).T.astype(BF16)

    @pl.when(i == 0)
    def _():
        ka_o[0] = _rms_heads64(ka_c[0].astype(F32), gk_ref[...]).astype(BF16)
        kc_o[0] = kc_c[0]
        put_values(va_c[0], vc_c[0])

    @pl.when(i > 0)
    def _():
        c, s = cos_ref[...], sin_ref[...]
        ka = _rms_heads64(ka_l[0].astype(F32), gk_ref[...])
        ka_o[0] = _rope128(ka, c, s).astype(BF16)
        for j in range(4):
            sl = slice(LANES * j, LANES * (j + 1))
            kc_o[0, :, sl] = _rope128(kc_l[0, :, sl].astype(F32), c, s).astype(BF16)
        put_values(va_l[0], vc_l[0])


def _prep_kv(pb_l, pb_c, cos_t, sin_t, gk128, tk):
    b, s, _ = pb_l.shape
    lc = pb_c.shape[1]
    tr = lc
    nb = (lc + s) // tr
    per = tk // tr
    nkv = (lc + s) // tk
    li = lambda i: jnp.maximum(i - 1, 0)
    lat = lambda col: (lambda bb, i: (bb, li(i), col))
    ctx = lambda col: (lambda bb, i: (bb, 0, col))
    wide, nar = (1, tr, 512), (1, tr, LANES)
    outw = lambda w: pl.BlockSpec((1, tr, w), lambda bb, i: (bb, i, 0))
    return pl.pallas_call(
        _prep_kv_kernel,
        grid=(b, nb),
        in_specs=[pl.BlockSpec(wide, lat(P2_KC // 512)), pl.BlockSpec(wide, lat(P2_VC // 512)),
                  pl.BlockSpec(nar, lat(P2_KA // LANES)), pl.BlockSpec(nar, lat(P2_VA // LANES)),
                  pl.BlockSpec(wide, ctx(P2_KC // 512)), pl.BlockSpec(wide, ctx(P2_VC // 512)),
                  pl.BlockSpec(nar, ctx(P2_KA // LANES)), pl.BlockSpec(nar, ctx(P2_VA // LANES)),
                  pl.BlockSpec((tr, LANES), lambda bb, i: (li(i), 0)),
                  pl.BlockSpec((tr, LANES), lambda bb, i: (li(i), 0)),
                  pl.BlockSpec((1, LANES), lambda bb, i: (0, 0))],
        out_specs=[outw(LANES),
                   pl.BlockSpec((1, 1, LANES, tr), lambda bb, i: (bb, i // per, 0, i % per)),
                   outw(512),
                   pl.BlockSpec((1, C_HEADS, 1, LANES, tr), lambda bb, i: (bb, 0, i // per, 0, i % per))],
        out_shape=[jax.ShapeDtypeStruct((b, lc + s, LANES), BF16),
                   jax.ShapeDtypeStruct((b, nkv, LANES, tk), BF16),
                   jax.ShapeDtypeStruct((b, lc + s, 512), BF16),
                   jax.ShapeDtypeStruct((b, C_HEADS, nkv, LANES, tk), BF16)],
        compiler_params=_cparams(("arbitrary", "arbitrary")),
        name="prep_kv",
    )(pb_l, pb_l, pb_l, pb_l, pb_c, pb_c, pb_c, pb_c, cos_t, sin_t, gk128)


def _pick_tk(kv):
    for tk in (768, 512, 384, 256, 128):
        if kv % tk == 0:
            return tk
    raise ValueError(f"key count {kv} must be a multiple of {LANES}")


def _flash_t(qs_ref, k_ref, vt_fn, tk, nkv):
    rows = qs_ref.shape[0]
    strip = min(QSTRIP, rows)
    ns = rows // strip
    qs = [qs_ref[n * strip:(n + 1) * strip, :] for n in range(ns)]
    nt = (((1,), (1,)), ((), ()))

    def scores(c, n):
        return lax.dot_general(k_ref[0, c * tk:(c + 1) * tk, :], qs[n], nt, preferred_element_type=F32)

    steps = [(c, n) for c in range(nkv) for n in range(ns)]
    ms, ls, accs = [None] * ns, [None] * ns, [None] * ns
    st = scores(0, 0)
    for i, (c, n) in enumerate(steps):
        st_next = scores(*steps[i + 1]) if i + 1 < len(steps) else None
        m_cur = jnp.max(st, axis=0, keepdims=True)
        if c == 0:
            m_new = m_cur
            p = jnp.exp2(st - m_new)
            ls[n] = jnp.sum(p, axis=0, keepdims=True)
            accs[n] = jnp.dot(vt_fn(c, n), p.astype(BF16), preferred_element_type=F32)
        else:
            m_new = jnp.maximum(ms[n], m_cur)
            alpha = jnp.exp2(ms[n] - m_new)
            p = jnp.exp2(st - m_new)
            ls[n] = alpha * ls[n] + jnp.sum(p, axis=0, keepdims=True)
            accs[n] = alpha * accs[n] + jnp.dot(vt_fn(c, n), p.astype(BF16), preferred_element_type=F32)
        ms[n] = m_new
        st = st_next
    return ls, accs


def _attn_a_kernel(q_ref, k_ref, vt_ref, cos_ref, sin_ref, gq_ref, o_ref, qs_ref, *, tq, tk, nkv, use_rope):
    lo = _lane_iota((tq, LANES)) < HEAD_DIM
    for j in range(4):
        x = _rms_heads64(q_ref[0, :, LANES * j:LANES * (j + 1)].astype(F32), gq_ref[...])
        if use_rope:
            x = _rope128(x, cos_ref[...], sin_ref[...])
        x = x * QSCALE
        xr = pltpu.roll(x, HEAD_DIM, 1)
        if j < 2:
            even, odd = jnp.where(lo, x, 0.0), jnp.where(lo, xr, 0.0)
        else:
            even, odd = jnp.where(lo, 0.0, xr), jnp.where(lo, 0.0, x)
        qs_ref[(2 * j) * tq:(2 * j + 1) * tq, :] = even.astype(BF16)
        qs_ref[(2 * j + 1) * tq:(2 * j + 2) * tq, :] = odd.astype(BF16)

    rows = A_HEADS * tq

    def vt_fn(c, n):
        if n * QSTRIP < rows // 2:
            return vt_ref[0, c, 0:HEAD_DIM, :]
        return vt_ref[0, c, HEAD_DIM:, :]

    ls, accs = _flash_t(qs_ref, k_ref, vt_fn, tk, nkv)

    o = jnp.concatenate([acc * (1.0 / l) for acc, l in zip(accs, ls)], axis=1)
    ot = jnp.concatenate([o[:, h * tq:(h + 1) * tq] for h in range(A_HEADS)], axis=0)
    o_ref[0] = ot.T.astype(BF16)


def _attn_a(pb_q, k_all, vt_all, cos_t, sin_t, gq128, kv_rows, use_rope):
    b, s, _ = pb_q.shape
    tq = QSTRIP * 2 // A_HEADS
    tk = min(vt_all.shape[-1], kv_rows)
    nkv = kv_rows // tk
    rows = A_HEADS * tq
    kern = functools.partial(_attn_a_kernel, tq=tq, tk=tk, nkv=nkv, use_rope=use_rope)
    return pl.pallas_call(
        kern,
        grid=(b, s // tq),
        in_specs=[pl.BlockSpec((1, tq, 512), lambda bb, i: (bb, i, P2_QA // 512)),
                  pl.BlockSpec((1, kv_rows, LANES), lambda bb, i: (bb, 0, 0)),
                  pl.BlockSpec((1, nkv, LANES, tk), lambda bb, i: (bb, 0, 0, 0)),
                  pl.BlockSpec((tq, LANES), lambda bb, i: (i, 0)),
                  pl.BlockSpec((tq, LANES), lambda bb, i: (i, 0)),
                  pl.BlockSpec((1, LANES), lambda bb, i: (0, 0))],
        out_specs=pl.BlockSpec((1, tq, 512), lambda bb, i: (bb, i, 0)),
        out_shape=jax.ShapeDtypeStruct((b, s, 512), BF16),
        scratch_shapes=[pltpu.VMEM((rows, LANES), BF16)],
        compiler_params=_cparams(("arbitrary", "arbitrary")),
        name="attn_gqa",
    )(pb_q, k_all, vt_all, cos_t, sin_t, gq128)


def _attn_c_kernel(q_ref, k_ref, vt_ref, cos_ref, sin_ref, dl_ref, sub_ref, o_ref, qs_ref,
                   *, tq, tk, nkv, use_rope, lam_init):
    lo = _lane_iota((tq, LANES)) < HEAD_DIM
    x = q_ref[0].astype(F32)
    if use_rope:
        x = _rope128(x, cos_ref[...], sin_ref[...])
    x = x * QSCALE
    qs_ref[0:tq, :] = jnp.where(lo, x, 0.0).astype(BF16)
    qs_ref[tq:2 * tq, :] = jnp.where(lo, 0.0, x).astype(BF16)

    ls, accs = _flash_t(qs_ref, k_ref, lambda c, n: vt_ref[0, 0, c], tk, nkv)

    dl = dl_ref[...]
    lam = (jnp.exp(jnp.sum(dl[0:1] * dl[1:2], axis=1, keepdims=True))
           - jnp.exp(jnp.sum(dl[2:3] * dl[3:4], axis=1, keepdims=True)) + lam_init)
    o = jnp.concatenate([acc * (1.0 / l) for acc, l in zip(accs, ls)], axis=1)
    y = (o[:, 0:tq] - lam * o[:, tq:2 * tq]).T
    ms = jnp.mean(y * y, axis=1, keepdims=True)
    y = ((y * lax.rsqrt(ms + RMS_EPS)) * sub_ref[...]) * (1.0 - lam_init)
    o_ref[0] = y.astype(BF16)


def _attn_c(pb_q, k_all, vt_all, cos_t, sin_t, dlam, subln, kv_rows, use_rope, lam_init):
    b, s, _ = pb_q.shape
    tq = min(QSTRIP, s)
    tk = min(vt_all.shape[-1], kv_rows)
    nkv = kv_rows // tk
    rows = 2 * tq
    kern = functools.partial(_attn_c_kernel, tq=tq, tk=tk, nkv=nkv, use_rope=use_rope, lam_init=lam_init)
    return pl.pallas_call(
        kern,
        grid=(b, C_HEADS, s // tq),
        in_specs=[pl.BlockSpec((1, tq, LANES), lambda bb, h, i: (bb, i, P2_QC // LANES + h)),
                  pl.BlockSpec((1, kv_rows, LANES), lambda bb, h, i: (bb, 0, h)),
                  pl.BlockSpec((1, 1, nkv, LANES, tk), lambda bb, h, i: (bb, h, 0, 0, 0)),
                  pl.BlockSpec((tq, LANES), lambda bb, h, i: (i, 0)),
                  pl.BlockSpec((tq, LANES), lambda bb, h, i: (i, 0)),
                  pl.BlockSpec((4, HEAD_DIM), lambda bb, h, i: (0, 0)),
                  pl.BlockSpec((1, LANES), lambda bb, h, i: (0, 0))],
        out_specs=pl.BlockSpec((1, tq, LANES), lambda bb, h, i: (bb, i, h)),
        out_shape=jax.ShapeDtypeStruct((b, s, 512), BF16),
        scratch_shapes=[pltpu.VMEM((rows, LANES), BF16)],
        compiler_params=_cparams(("arbitrary", "arbitrary", "arbitrary")),
        name="attn_diff",
    )(pb_q, k_all, vt_all, cos_t, sin_t, dlam, subln)


def _gelu_tanh(x):
    return 0.5 * x * (1.0 + jnp.tanh(math.sqrt(2.0 / math.pi) * (x + 0.044715 * (x * x * x))))


def _lru_kernel(*refs, nb, t, nchunks, reverse, final):
    (x_ref, xp_ref, xn_ref, h0_ref, cw_ref, cb_ref, wa_ref, ba_ref, wx_ref, bx_ref, lam_ref) = refs[:11]
    if final:
        gb_ref, hp_ref, y_ref, hl_ref, a_s, b_s, o_s, h_s = refs[11:]
    else:
        y_ref, hl_ref, a_s, b_s, o_s, h_s = refs[11:]
    c = pl.program_id(0)
    pos = (nchunks - 1 - c) if reverse else c

    @pl.when(c == 0)
    def _():
        h_s[...] = h0_ref[...]

    z = -lam_ref[...]
    softplus = jnp.maximum(z, 0.0) + jnp.log(1.0 + jnp.exp(-jnp.abs(z)))
    cw = cw_ref[...]
    for b in range(nb):
        x = x_ref[b]
        prev = jnp.where(pos == 0, 0.0, xp_ref[b])
        nxt = jnp.where(pos == nchunks - 1, 0.0, xn_ref[b])
        ext = jnp.concatenate([prev, x, nxt], axis=0)
        u = (cw[0:1] * ext[6:6 + t] + cw[1:2] * ext[7:7 + t] + cw[2:3] * x
             + cw[3:4] * ext[9:9 + t]) + cb_ref[...]
        ub = u.astype(BF16)
        r = jax.nn.sigmoid(jnp.dot(ub, wa_ref[...], preferred_element_type=F32) + ba_ref[...])
        gi = jax.nn.sigmoid(jnp.dot(ub, wx_ref[...], preferred_element_type=F32) + bx_ref[...])
        log_a = (-LRU_C * r) * softplus
        a = jnp.exp(log_a)
        mult = jnp.sqrt(-jnp.tanh(log_a) * (a * a + 1.0))
        a_s[b] = a
        b_s[b] = mult * (gi * u)

    ngroups = t // SUBLANES
    sub = lax.broadcasted_iota(I32, (SUBLANES, LRU_WIDTH), 0)
    order = range(SUBLANES - 1, -1, -1) if reverse else range(SUBLANES)

    def group(gidx, hs):
        g = (ngroups - 1 - gidx) if reverse else gidx
        base = pl.multiple_of(g * SUBLANES, SUBLANES)
        new = []
        for b in range(nb):
            ta = a_s[b, pl.ds(base, SUBLANES), :]
            tb = b_s[b, pl.ds(base, SUBLANES), :]
            h = hs[b]
            out = jnp.zeros((SUBLANES, LRU_WIDTH), F32)
            for r in order:
                h = (jnp.broadcast_to(ta[r:r + 1], h.shape) * h
                     + jnp.broadcast_to(tb[r:r + 1], h.shape))
                out = jnp.where(sub == r, h, out)
            o_s[b, pl.ds(base, SUBLANES), :] = out
            new.append(h)
        return tuple(new)

    hs = lax.fori_loop(0, ngroups, group, tuple(h_s[b] for b in range(nb)))
    for b in range(nb):
        h_s[b] = hs[b]
        if final:
            y = hp_ref[b] + o_s[b]
            y_ref[b] = (y * _gelu_tanh(gb_ref[b].astype(F32))).astype(y_ref.dtype)
        else:
            y_ref[b] = o_s[b]
    hl_ref[...] = h_s[...]


def _lru_pass(p1, h0, prm, d, reverse, gb_src=None, hprev=None):
    nb, length, _ = p1.shape
    t = min(256, length)
    nchunks = length // t
    final = gb_src is not None
    blk8 = t // SUBLANES
    nblk8 = length // SUBLANES
    xcol = P1_XB // LRU_WIDTH
    cpos = (lambda c: nchunks - 1 - c) if reverse else (lambda c: c)
    vec = lambda: pl.BlockSpec((1, LRU_WIDTH), lambda c: (0, 0))
    mat = lambda: pl.BlockSpec((LRU_WIDTH, LRU_WIDTH), lambda c: (0, 0))
    in_specs = [pl.BlockSpec((nb, t, LRU_WIDTH), lambda c: (0, cpos(c), xcol)),
                pl.BlockSpec((nb, SUBLANES, LRU_WIDTH),
                             lambda c: (0, jnp.maximum(cpos(c) * blk8 - 1, 0), xcol)),
                pl.BlockSpec((nb, SUBLANES, LRU_WIDTH),
                             lambda c: (0, jnp.minimum((cpos(c) + 1) * blk8, nblk8 - 1), xcol)),
                pl.BlockSpec((nb, SUBLANES, LRU_WIDTH), lambda c: (0, 0, 0)),
                pl.BlockSpec((CONV_WIDTH, LRU_WIDTH), lambda c: (0, 0)), vec(),
                mat(), vec(), mat(), vec(), vec()]
    args = [p1, p1, p1, h0, prm["conv_w"], prm["conv_b"], prm["wa"][d], prm["ba"][d],
            prm["wx"][d], prm["bx"][d], prm["lam"][d]]
    if final:
        in_specs += [pl.BlockSpec((nb, t, LRU_WIDTH), lambda c: (0, cpos(c), P2_GB // LRU_WIDTH)),
                     pl.BlockSpec((nb, t, LRU_WIDTH), lambda c: (0, cpos(c), 0))]
        args += [gb_src, hprev]
    kern = functools.partial(_lru_kernel, nb=nb, t=t, nchunks=nchunks, reverse=reverse, final=final)
    return pl.pallas_call(
        kern,
        grid=(nchunks,),
        in_specs=in_specs,
        out_specs=[pl.BlockSpec((nb, t, LRU_WIDTH), lambda c: (0, cpos(c), 0)),
                   pl.BlockSpec((nb, SUBLANES, LRU_WIDTH), lambda c: (0, 0, 0))],
        out_shape=[jax.ShapeDtypeStruct((nb, length, LRU_WIDTH), BF16 if final else F32),
                   jax.ShapeDtypeStruct((nb, SUBLANES, LRU_WIDTH), F32)],
        scratch_shapes=[pltpu.VMEM((nb, t, LRU_WIDTH), F32), pltpu.VMEM((nb, t, LRU_WIDTH), F32),
                        pltpu.VMEM((nb, t, LRU_WIDTH), F32), pltpu.VMEM((nb, SUBLANES, LRU_WIDTH), F32)],
        compiler_params=_cparams(("arbitrary",)),
        name="lru_bwd" if reverse else "lru_fwd",
    )(*args)


def _merge_kernel(ya_ref, yb_ref, yc_ref, gm_ref, x_ref, g1_ref, sc2_ref, sh2_ref, lg_ref, lb_ref,
                  wb_ref, wo_ref, wr_ref, x1_ref, h2_ref, aff_ref):
    acc = None
    for n, br in enumerate((ya_ref, yb_ref, yc_ref)):
        p = jnp.dot(br[0], wb_ref[n], preferred_element_type=F32)
        gate = jax.nn.sigmoid(gm_ref[0, :, n * D_MODEL:(n + 1) * D_MODEL])
        acc = gate * p if acc is None else acc + gate * p
    o = jnp.dot(acc.astype(BF16), wo_ref[...], preferred_element_type=F32)
    x1 = _layer_norm(DN_ALPHA * x_ref[0] + g1_ref[0] * o, lg_ref[...], lb_ref[...])
    x1_ref[0] = x1
    h2 = x1 * (1.0 + sc2_ref[0]) + sh2_ref[0]
    h2_ref[0] = h2
    logits = jnp.dot(h2.astype(BF16), wr_ref[...], preferred_element_type=F32)
    logits = jnp.where(_lane_iota(logits.shape) < N_EXPERTS, logits, -jnp.inf)
    e = jnp.exp(logits - jnp.max(logits, axis=-1, keepdims=True))
    aff_ref[0] = e / jnp.sum(e, axis=-1, keepdims=True)


def _merge(ya, yb, yc, p1, x, g1, sc2, sh2, lg, lb, wb, wo, wr):
    b, s, d = x.shape
    tm = min(512, s)
    row = lambda w, col=0: pl.BlockSpec((1, tm, w), lambda bb, i: (bb, i, col))
    per_b = lambda: pl.BlockSpec((1, 1, d), lambda bb, i: (bb, 0, 0))
    full = lambda shape: pl.BlockSpec(shape, lambda bb, i: (0,) * len(shape))
    return pl.pallas_call(
        _merge_kernel,
        grid=(b, s // tm),
        in_specs=[row(512), row(512), row(512), row(N_BRANCH * d, P1_GM // (N_BRANCH * d)), row(d),
                  per_b(), per_b(), per_b(), full((1, d)), full((1, d)),
                  full((N_BRANCH, BRANCH_W, d)), full((d, d)), full((d, LANES))],
        out_specs=[row(d), row(d), row(LANES)],
        out_shape=[jax.ShapeDtypeStruct((b, s, d), F32), jax.ShapeDtypeStruct((b, s, d), F32),
                   jax.ShapeDtypeStruct((b, s, LANES), F32)],
        compiler_params=_cparams(("arbitrary", "arbitrary")),
        name="merge_ln1_router",
    )(ya, yb, yc, p1, x, g1, sc2, sh2, lg, lb, wb, wo, wr)


def _route_kernel(aff_ref, pos_ref, base_ref, cnt_ref, x_s, c_s, *, n, cap, blk):
    v = aff_ref[0]
    kf = float(cap)

    def bit_body(i, thr):
        cand = thr | lax.shift_left(jnp.int32(1), 30 - i)
        hit = jnp.where(v >= pltpu.bitcast(cand, F32), 1.0, 0.0)
        return jnp.where(jnp.sum(hit, axis=0, keepdims=True) >= kf, cand, thr)

    thr = pltpu.bitcast(lax.fori_loop(0, 31, bit_body, jnp.zeros((1, LANES), I32)), F32)
    gt = v > thr
    eq = v == thr
    need = kf - jnp.sum(jnp.where(gt, 1.0, 0.0), axis=0, keepdims=True)

    nblk = n // blk
    tri = (lax.broadcasted_iota(I32, (blk, blk), 0) > lax.broadcasted_iota(I32, (blk, blk), 1))
    tri = jnp.where(tri, 1.0, 0.0).astype(BF16)

    def excl_cumsum(write_base):
        carry = jnp.zeros((1, LANES), F32)
        for j in range(nblk):
            xb = x_s[j * blk:(j + 1) * blk, :]
            c_s[j * blk:(j + 1) * blk, :] = (
                jnp.dot(tri, xb.astype(BF16), preferred_element_type=F32) + carry)
            if write_base:
                base_ref[0, j:j + 1, :] = carry.astype(I32)
            carry = carry + jnp.sum(xb, axis=0, keepdims=True)

    x_s[...] = jnp.where(eq, 1.0, 0.0)
    excl_cumsum(False)
    take_eq = jnp.where(c_s[...] < need, 1.0, 0.0)
    sel = jnp.where(gt, 1.0, jnp.where(eq, take_eq, 0.0))
    sel = jnp.where(_lane_iota(sel.shape) < N_EXPERTS, sel, 0.0)
    x_s[...] = sel
    excl_cumsum(True)
    pos_ref[0] = jnp.where(x_s[...] > 0.5, c_s[...].astype(I32), -1)
    cnt_ref[0] = (c_s[...] + x_s[...]).astype(I32)


def _route(aff, cap, blk):
    b, n, _ = aff.shape
    kern = functools.partial(_route_kernel, n=n, cap=cap, blk=blk)
    tok = lambda: pl.BlockSpec((1, n, LANES), lambda bb: (bb, 0, 0))
    return pl.pallas_call(
        kern,
        grid=(b,),
        in_specs=[tok()],
        out_specs=[tok(), pl.BlockSpec((1, n // blk, LANES), lambda bb: (bb, 0, 0)), tok()],
        out_shape=[jax.ShapeDtypeStruct((b, n, LANES), I32),
                   jax.ShapeDtypeStruct((b, n // blk, LANES), I32),
                   jax.ShapeDtypeStruct((b, n, LANES), I32)],
        scratch_shapes=[pltpu.VMEM((n, LANES), F32), pltpu.VMEM((n, LANES), F32)],
        compiler_params=_cparams(("arbitrary",)),
        name="route",
    )(aff)


def _slot_index_kernel(base_ref, cnt_ref, idx_ref, acc_s, add_s, *, nblk, cap):
    b = pl.program_id(0)
    ncol = max(cap // LANES, 1)
    pw = min(cap, LANES)
    lane = _lane_iota((LANES, LANES))
    for e in range(N_EXPERTS):
        acc_s[...] = jnp.zeros(acc_s.shape, F32)
        add_s[...] = jnp.zeros(add_s.shape, F32)

        def block(r, carry, e=e):
            k0 = lax.shift_right_logical(base_ref[(b * nblk + r) * N_EXPERTS + e], 7)
            off = pl.multiple_of(r * LANES, LANES)
            cb = jnp.broadcast_to(cnt_ref[0, pl.ds(off, LANES), e:e + 1], (LANES, LANES))
            for dk in range(2):
                hit = jnp.where(cb <= lane + (k0 + dk) * LANES, 1.0, 0.0)
                acc_s[k0 + dk] += jnp.sum(hit.reshape(-1, SUBLANES, LANES), axis=0)
            add_s[k0 + 2] += float(LANES // SUBLANES)
            return carry

        lax.fori_loop(0, nblk, block, 0, unroll=min(8, nblk))
        run = jnp.zeros((SUBLANES, LANES), F32)
        for k in range(ncol):
            run = run + add_s[k]
            tot = jnp.sum(acc_s[k] + run, axis=0, keepdims=True)
            idx_ref[0, e:e + 1, LANES * k:LANES * k + pw] = tot[:, :pw].astype(I32)


def _slot_index(base_flat, cnt, cap):
    b, n, _ = cnt.shape
    nblk = n // LANES
    ncol = max(cap // LANES, 1)
    kern = functools.partial(_slot_index_kernel, nblk=nblk, cap=cap)
    grid_spec = pltpu.PrefetchScalarGridSpec(
        num_scalar_prefetch=1,
        grid=(b,),
        in_specs=[pl.BlockSpec((1, n, LANES), lambda bb, base: (bb, 0, 0))],
        out_specs=pl.BlockSpec((1, N_EXPERTS, cap), lambda bb, base: (bb, 0, 0)),
        scratch_shapes=[pltpu.VMEM((ncol + 2, SUBLANES, LANES), F32),
                        pltpu.VMEM((ncol + 3, SUBLANES, LANES), F32)],
    )
    return pl.pallas_call(
        kern,
        grid_spec=grid_spec,
        out_shape=jax.ShapeDtypeStruct((b, N_EXPERTS, cap), I32),
        compiler_params=_cparams(("arbitrary",)),
        name="slot_index",
    )(base_flat, cnt)


def _expert_kernel(idx_ref, h2_hbm, wg_ref, wu_ref, wd_ref, y_ref, xbuf, sem, *, nb, ne, cap, tsub):
    e = pl.program_id(0)
    b = pl.program_id(1)
    step = e * nb + b
    slot = step % 2

    def gather(st, sl):
        bb = st % nb
        off = (bb * ne + st // nb) * cap

        def body(r, carry):
            tok = idx_ref[off + r]
            pltpu.make_async_copy(h2_hbm.at[bb, pl.ds(tok, 1), :], xbuf.at[sl, pl.ds(r, 1), :],
                                  sem.at[sl]).start()
            return carry

        lax.fori_loop(0, cap, body, 0, unroll=8)

    @pl.when(step == 0)
    def _():
        gather(step, slot)

    @pl.when(step + 1 < nb * ne)
    def _():
        gather(step + 1, 1 - slot)

    pltpu.make_async_copy(xbuf.at[slot], xbuf.at[slot], sem.at[slot]).wait()

    for s in range(cap // tsub):
        x = xbuf[slot, s * tsub:(s + 1) * tsub, :].astype(BF16)
        g = jnp.dot(x, wg_ref[0, 0], preferred_element_type=F32)
        u = jnp.dot(x, wu_ref[0, 0], preferred_element_type=F32)
        act = ((g * jax.nn.sigmoid(g)) * u).astype(BF16)
        y_ref[0, 0, s * tsub:(s + 1) * tsub, :] = jnp.dot(
            act, wd_ref[0, 0], preferred_element_type=F32).astype(y_ref.dtype)


def _experts(idx_flat, h2, wg, wu, wd, l, cap):
    nb, n, d = h2.shape
    _, ne, _, fp = wg.shape
    tsub = min(256, cap)
    kern = functools.partial(_expert_kernel, nb=nb, ne=ne, cap=cap, tsub=tsub)
    grid_spec = pltpu.PrefetchScalarGridSpec(
        num_scalar_prefetch=1,
        grid=(ne, nb),
        in_specs=[pl.BlockSpec(memory_space=pl.ANY),
                  pl.BlockSpec((1, 1, d, fp), lambda e, b, idx: (l, e, 0, 0)),
                  pl.BlockSpec((1, 1, d, fp), lambda e, b, idx: (l, e, 0, 0)),
                  pl.BlockSpec((1, 1, fp, d), lambda e, b, idx: (l, e, 0, 0))],
        out_specs=pl.BlockSpec((1, 1, cap, d), lambda e, b, idx: (b, e, 0, 0)),
        scratch_shapes=[pltpu.VMEM((2, cap, d), F32), pltpu.SemaphoreType.DMA((2,))],
    )
    return pl.pallas_call(
        kern,
        grid_spec=grid_spec,
        out_shape=jax.ShapeDtypeStruct((nb, ne, cap, d), BF16),
        compiler_params=_cparams(("arbitrary", "arbitrary"), 60 * 1024 * 1024),
        name="expert_ffn",
    )(idx_flat, h2, wg, wu, wd)


def _combine_kernel(base_ref, x1_ref, pos_ref, aff_ref, g2_ref, lg_ref, lb_ref, y_hbm, o_ref, ybuf, sem,
                    *, nb, nt, ne, cap, tt, win):
    b = pl.program_id(0)
    i = pl.program_id(1)
    step = b * nt + i
    slot = step % 2

    def win_start(st, e):
        base = base_ref[st * ne + e]
        al = lax.shift_left(lax.shift_right_logical(base, 4), 4)
        return pl.multiple_of(jnp.minimum(al, cap - win), BF16_ROWS)

    def fetch(st, sl):
        bb = st // nt
        for e in range(ne):
            pltpu.make_async_copy(y_hbm.at[bb, e, pl.ds(win_start(st, e), win), :], ybuf.at[sl, e],
                                  sem.at[sl]).start()

    @pl.when(step == 0)
    def _():
        fetch(step, slot)

    @pl.when(step + 1 < nb * nt)
    def _():
        fetch(step + 1, 1 - slot)

    for e in range(ne):
        pltpu.make_async_copy(y_hbm.at[0, e, pl.ds(0, win), :], ybuf.at[slot, e], sem.at[slot]).wait()

    pos = pos_ref[0]
    aff = aff_ref[0]
    col = _lane_iota((tt, win))
    acc = jnp.zeros((tt, D_MODEL), F32)
    for e in range(ne):
        rel = pos[:, e:e + 1] - win_start(step, e)
        onehot = jnp.where(col == rel, 1.0, 0.0).astype(BF16)
        acc = acc + aff[:, e:e + 1] * jnp.dot(onehot, ybuf[slot, e], preferred_element_type=F32)
    o_ref[0] = _layer_norm(DN_ALPHA * x1_ref[0] + g2_ref[0] * acc, lg_ref[...], lb_ref[...])


def _combine(base_flat, x1, pos, aff, g2, lg, lb, y, tt):
    nb, n, d = x1.shape
    ne, cap = y.shape[1], y.shape[2]
    nt = n // tt
    win = min(tt + BF16_ROWS, cap)
    kern = functools.partial(_combine_kernel, nb=nb, nt=nt, ne=ne, cap=cap, tt=tt, win=win)
    row = lambda w: pl.BlockSpec((1, tt, w), lambda b, i, base: (b, i, 0))
    grid_spec = pltpu.PrefetchScalarGridSpec(
        num_scalar_prefetch=1,
        grid=(nb, nt),
        in_specs=[row(d), row(LANES), row(LANES),
                  pl.BlockSpec((1, 1, d), lambda b, i, base: (b, 0, 0)),
                  pl.BlockSpec((1, d), lambda b, i, base: (0, 0)),
                  pl.BlockSpec((1, d), lambda b, i, base: (0, 0)),
                  pl.BlockSpec(memory_space=pl.ANY)],
        out_specs=row(d),
        scratch_shapes=[pltpu.VMEM((2, ne, win, d), BF16), pltpu.SemaphoreType.DMA((2,))],
    )
    return pl.pallas_call(
        kern,
        grid_spec=grid_spec,
        out_shape=jax.ShapeDtypeStruct((nb, n, d), F32),
        compiler_params=_cparams(("arbitrary", "arbitrary")),
        name="combine_ln2",
    )(base_flat, x1, pos, aff, g2, lg, lb, y)


def _moe(x1, h2, aff, g2, lg, lb, wg, wu, wd, l):
    nb, n, d = x1.shape
    cap = EC_FACTOR * n // N_EXPERTS
    tt = min(128, n)
    assert tt == min(LANES, n)
    pos, base, cnt = _route(aff, cap, tt)
    base_flat = base[:, :, :N_EXPERTS].reshape(-1)
    idx = _slot_index(base_flat, cnt, cap)
    y = _experts(idx.reshape(-1), h2, wg, wu, wd, l, cap)
    return _combine(base_flat, x1, pos, aff, g2, lg, lb, y, tt)


def _rope_tables(seq):
    rows = seq // GRID_W
    row = jnp.repeat(jnp.arange(rows, dtype=I32), GRID_W)
    colm = jnp.tile(jnp.arange(GRID_W, dtype=I32), rows)
    inv = ROPE_THETA ** (-jnp.arange(N_FREQ, dtype=F32) / N_FREQ)
    ang = jnp.stack([row, colm], -1).astype(F32)[..., None] * inv
    cos, sin = jnp.cos(ang), jnp.sin(ang)
    c64 = jnp.concatenate([cos[:, 0], cos[:, 0], cos[:, 1], cos[:, 1]], axis=-1)
    s64 = jnp.concatenate([-sin[:, 0], sin[:, 0], -sin[:, 1], sin[:, 1]], axis=-1)
    return jnp.tile(c64, (1, 2)), jnp.tile(s64, (1, 2))


def _block_diag(w):
    nblk, bw, _ = w.shape
    eye = jnp.eye(nblk, dtype=w.dtype)
    return (eye[:, None, :, None] * w[:, :, None, :]).reshape(nblk * bw, nblk * bw)


def kernel(x, c, ctx, c_ctx, w_ada, b_ada, w_in, a_q_norm, a_k_norm, lru_conv_w, lru_conv_b, lru_w_a,
           lru_b_a, lru_w_x, lru_b_x, lru_lambda, diff_lambda, diff_subln, w_branch, w_out, ln1_g, ln1_b,
           w_router, w_gate, w_up, w_down, ln2_g, ln2_b):
    nb, seq, d = x.shape
    lc = ctx.shape[1]
    depth = w_in.shape[0]
    assert d == D_MODEL and nb + 1 <= SUBLANES and seq % lc == 0 and lc % LANES == 0

    cos_t, sin_t = _rope_tables(seq)
    cc = jnp.zeros((SUBLANES, d), F32).at[:nb].set(c).at[nb].set(c_ctx)
    mod_all = _ada(cc, w_ada, b_ada)

    zeros8 = jnp.zeros((nb, SUBLANES, LRU_WIDTH), F32)

    w1 = jnp.concatenate([w_in[:, :, 3328:6400], w_in[:, :, 768:1280]], axis=2).astype(BF16)
    w2 = jnp.concatenate([w_in[:, :, 0:512], w_in[:, :, 1280:1792], w_in[:, :, 1792:2304],
                          w_in[:, :, 2304:2816], w_in[:, :, 2816:3328], w_in[:, :, 512:640],
                          w_in[:, :, 640:768]], axis=2).astype(BF16)
    fexp = w_gate.shape[-1]
    fpad = -(-fexp // MXU_DIM) * MXU_DIM
    padf = ((0, 0), (0, 0), (0, 0), (0, fpad - fexp))
    wg = jnp.pad(w_gate.astype(BF16), padf)
    wu = jnp.pad(w_up.astype(BF16), padf)
    wd = jnp.pad(w_down.astype(BF16), ((0, 0), (0, 0), (0, fpad - fexp), (0, 0)))

    for l in range(depth):
        need_ctx = l < depth - 1
        lam_init = 0.8 - 0.6 * math.exp(-0.3 * l)
        mod = mod_all[l]
        m6 = mod[:nb].reshape(nb, 6, 1, d)
        sh1, sc1, g1, sh2, sc2, g2 = (m6[:, k] for k in range(6))
        mc6 = jnp.broadcast_to(mod[nb].reshape(1, 6, 1, d), (nb, 6, 1, d))
        sh1c, sc1c, g1c, sh2c, sc2c, g2c = (mc6[:, k] for k in range(6))

        gq128 = jnp.tile(a_q_norm[l], 2).reshape(1, LANES)
        gk128 = jnp.tile(a_k_norm[l], 2).reshape(1, LANES)
        lru = {
            "conv_w": lru_conv_w[l], "conv_b": lru_conv_b[l].reshape(1, -1),
            "wa": [_block_diag(lru_w_a[l, k]).astype(BF16) for k in range(2)],
            "wx": [_block_diag(lru_w_x[l, k]).astype(BF16) for k in range(2)],
            "ba": [lru_b_a[l, k].reshape(1, -1) for k in range(2)],
            "bx": [lru_b_x[l, k].reshape(1, -1) for k in range(2)],
            "lam": [lru_lambda[l, k].reshape(1, -1) for k in range(2)],
        }
        subln = diff_subln[l].reshape(1, LANES)
        wb = w_branch[l].astype(BF16)
        wo = w_out[l].astype(BF16)
        wr = jnp.zeros((d, LANES), F32).at[:, :N_EXPERTS].set(w_router[l]).astype(BF16)
        lg1, lb1 = ln1_g[l].reshape(1, d), ln1_b[l].reshape(1, d)
        lg2, lb2 = ln2_g[l].reshape(1, d), ln2_b[l].reshape(1, d)

        p1 = _mod_matmul(x, sc1, sh1, w1, l, F32, P1_W, "inproj_f32")
        p2 = _mod_matmul(x, sc1, sh1, w2, l, BF16, P2_W, "inproj_bf16")
        p1c = _mod_matmul(ctx, sc1c, sh1c, w1, l, F32, P1_W, "inproj_f32")
        p2c = _mod_matmul(ctx, sc1c, sh1c, w2, l, BF16, P2_W, "inproj_bf16")

        kv = lc + seq
        ka, vat, kc, vct = _prep_kv(p2, p2c, cos_t, sin_t, gk128, _pick_tk(kv))
        ya = _attn_a(p2, ka, vat, cos_t, sin_t, gq128, kv, True)
        yc = _attn_c(p2, kc, vct, cos_t, sin_t, diff_lambda[l], subln, kv, True, lam_init)

        hf_c, hl_f = _lru_pass(p1c, zeros8, lru, 0, False)
        yb_c, hl_b = _lru_pass(p1c, zeros8, lru, 1, True, gb_src=p2c, hprev=hf_c)
        hf, _ = _lru_pass(p1, hl_f, lru, 0, False)
        yb, _ = _lru_pass(p1, hl_b, lru, 1, True, gb_src=p2, hprev=hf)

        x1, h2, aff = _merge(ya, yb, yc, p1, x, g1, sc2, sh2, lg1, lb1, wb, wo, wr)
        x_next = _moe(x1, h2, aff, g2, lg2, lb2, wg, wu, wd, l)

        if need_ctx:
            ya_c = _attn_a(p2c, ka, vat, cos_t, sin_t, gq128, lc, False)
            yc_c = _attn_c(p2c, kc, vct, cos_t, sin_t, diff_lambda[l], subln, lc, False, lam_init)
            c1, hc2, affc = _merge(ya_c, yb_c, yc_c, p1c, ctx, g1c, sc2c, sh2c, lg1, lb1, wb, wo, wr)
            ctx = _moe(c1, hc2, affc, g2c, lg2, lb2, wg, wu, wd, l)
        x = x_next
    return x
```

```python
import functools
import math

import jax
import jax.numpy as jnp
from jax import lax
from jax.experimental import pallas as pl
from jax.experimental.pallas import tpu as pltpu

F32 = jnp.float32
BF16 = jnp.bfloat16
I32 = jnp.int32

D_MODEL = 1024
DEPTH = 2
GRID_W = 64
HEAD_DIM = 64
N_FREQ = HEAD_DIM // 4
ROPE_THETA = 10000.0
A_HEADS = 8
A_KV_HEADS = 2
LRU_WIDTH = 512
LRU_BLOCKS = 8
CONV_WIDTH = 4
LRU_C = 8.0
C_HEADS = 4
N_BRANCH = 3
BRANCH_W = 512
N_EXPERTS = 16
EC_FACTOR = 2
DN_ALPHA = (2 * DEPTH) ** 0.25
LN_EPS = 1e-5
RMS_EPS = 1e-6

LANES = 128
SUBLANES = 8
BF16_ROWS = 16
MXU_DIM = 256
QSTRIP = 2 * MXU_DIM
VMEM_LIMIT = 56 * 1024 * 1024

LOG2E = 1.4426950408889634
QSCALE = (HEAD_DIM ** -0.5) * LOG2E

P1_GM = 0
P1_XB = N_BRANCH * D_MODEL
P1_W = P1_XB + LRU_WIDTH
P2_QA, P2_GB, P2_QC, P2_KC, P2_VC, P2_KA, P2_VA = 0, 512, 1024, 1536, 2048, 2560, 2688
P2_W = 2816


def _cparams(sem, vmem=VMEM_LIMIT):
    return pltpu.CompilerParams(dimension_semantics=sem, vmem_limit_bytes=vmem)


def _lane_iota(shape):
    return lax.broadcasted_iota(I32, shape, len(shape) - 1)


def _layer_norm(z, g, b):
    mu = jnp.mean(z, axis=-1, keepdims=True)
    zc = z - mu
    var = jnp.mean(zc * zc, axis=-1, keepdims=True)
    return (zc * lax.rsqrt(var + LN_EPS)) * g + b


def _rms_heads64(x, g128):
    lo = _lane_iota(x.shape) < HEAD_DIM
    x2 = x * x
    s_lo = jnp.sum(jnp.where(lo, x2, 0.0), axis=-1, keepdims=True)
    s_hi = jnp.sum(jnp.where(lo, 0.0, x2), axis=-1, keepdims=True)
    inv = jnp.where(lo, lax.rsqrt(s_lo * (1.0 / HEAD_DIM) + RMS_EPS),
                    lax.rsqrt(s_hi * (1.0 / HEAD_DIM) + RMS_EPS))
    return (x * inv) * g128


def _rope128(x, c, s):
    first = (_lane_iota(x.shape) & (2 * N_FREQ - 1)) < N_FREQ
    partner = jnp.where(first, pltpu.roll(x, LANES - N_FREQ, 1), pltpu.roll(x, N_FREQ, 1))
    return x * c + partner * s


def _ada_kernel(c_ref, w_ref, b_ref, o_ref):
    c = c_ref[...]
    s = c * jax.nn.sigmoid(c)
    o_ref[0] = jnp.dot(s.astype(BF16), w_ref[0].astype(BF16), preferred_element_type=F32) + b_ref[0]


def _ada(cc, w_ada, b_ada):
    nl, d, n6 = w_ada.shape
    tn = 1536
    return pl.pallas_call(
        _ada_kernel,
        grid=(nl, n6 // tn),
        in_specs=[pl.BlockSpec((SUBLANES, d), lambda l, j: (0, 0)),
                  pl.BlockSpec((1, d, tn), lambda l, j: (l, 0, j)),
                  pl.BlockSpec((1, 1, tn), lambda l, j: (l, 0, j))],
        out_specs=pl.BlockSpec((1, SUBLANES, tn), lambda l, j: (l, 0, j)),
        out_shape=jax.ShapeDtypeStruct((nl, SUBLANES, n6), F32),
        compiler_params=_cparams(("arbitrary", "arbitrary")),
        name="ada",
    )(cc, w_ada, b_ada.reshape(nl, 1, n6))


def _mod_matmul_kernel(x_ref, sc_ref, sh_ref, w_ref, o_ref):
    h = x_ref[0] * (1.0 + sc_ref[0]) + sh_ref[0]
    o_ref[0] = jnp.dot(h.astype(BF16), w_ref[0], preferred_element_type=F32).astype(o_ref.dtype)


def _mod_matmul(x, sc, sh, w, l, out_dtype, tn, name):
    g, r, k = x.shape
    n = w.shape[2]
    tm = min(512, r)
    return pl.pallas_call(
        _mod_matmul_kernel,
        grid=(g, r // tm, n // tn),
        in_specs=[pl.BlockSpec((1, tm, k), lambda a, i, j: (a, i, 0)),
                  pl.BlockSpec((1, 1, k), lambda a, i, j: (a, 0, 0)),
                  pl.BlockSpec((1, 1, k), lambda a, i, j: (a, 0, 0)),
                  pl.BlockSpec((1, k, tn), lambda a, i, j: (l, 0, j))],
        out_specs=pl.BlockSpec((1, tm, tn), lambda a, i, j: (a, i, j)),
        out_shape=jax.ShapeDtypeStruct((g, r, n), out_dtype),
        compiler_params=_cparams(("arbitrary", "arbitrary", "arbitrary")),
        name=name,
    )(x, sc, sh, w)


def _prep_kv_kernel(kc_l, vc_l, ka_l, va_l, kc_c, vc_c, ka_c, va_c, cos_ref, sin_ref, gk_ref,
                    ka_o, vat_o, kc_o, vct_o):
    i = pl.program_id(1)

    def put_values(va, vc):
        vat_o[0, 0] = va.astype(F32).T.astype(BF16)
        for h in range(C_HEADS):
            vct_o[0, h, 0] = vc[:, LANES * h:LANES * (h + 1)].astype(F32).T.astype(BF16)

    @pl.when(i == 0)
    def _():
        ka_o[0] = _rms_heads64(ka_c[0].astype(F32), gk_ref[...]).astype(BF16)
        kc_o[0] = kc_c[0]
        put_values(va_c[0], vc_c[0])

    @pl.when(i > 0)
    def _():
        c, s = cos_ref[...], sin_ref[...]
        ka = _rms_heads64(ka_l[0].astype(F32), gk_ref[...])
        ka_o[0] = _rope128(ka, c, s).astype(BF16)
        for j in range(4):
            sl = slice(LANES * j, LANES * (j + 1))
            kc_o[0, :, sl] = _rope128(kc_l[0, :, sl].astype(F32), c, s).astype(BF16)
        put_values(va_l[0], vc_l[0])


def _prep_kv(pb_l, pb_c, cos_t, sin_t, gk128, tk):
    b, s, _ = pb_l.shape
    lc = pb_c.shape[1]
    tr = lc
    nb = (lc + s) // tr
    per = tk // tr
    nkv = (lc + s) // tk
    li = lambda i: jnp.maximum(i - 1, 0)
    lat = lambda col: (lambda bb, i: (bb, li(i), col))
    ctx = lambda col: (lambda bb, i: (bb, 0, col))
    wide, nar = (1, tr, 512), (1, tr, LANES)
    outw = lambda w: pl.BlockSpec((1, tr, w), lambda bb, i: (bb, i, 0))
    return pl.pallas_call(
        _prep_kv_kernel,
        grid=(b, nb),
        in_specs=[pl.BlockSpec(wide, lat(P2_KC // 512)), pl.BlockSpec(wide, lat(P2_VC // 512)),
                  pl.BlockSpec(nar, lat(P2_KA // LANES)), pl.BlockSpec(nar, lat(P2_VA // LANES)),
                  pl.BlockSpec(wide, ctx(P2_KC // 512)), pl.BlockSpec(wide, ctx(P2_VC // 512)),
                  pl.BlockSpec(nar, ctx(P2_KA // LANES)), pl.BlockSpec(nar, ctx(P2_VA // LANES)),
                  pl.BlockSpec((tr, LANES), lambda bb, i: (li(i), 0)),
                  pl.BlockSpec((tr, LANES), lambda bb, i: (li(i), 0)),
                  pl.BlockSpec((1, LANES), lambda bb, i: (0, 0))],
        out_specs=[outw(LANES),
                   pl.BlockSpec((1, 1, LANES, tr), lambda bb, i: (bb, i // per, 0, i % per)),
                   outw(512),
                   pl.BlockSpec((1, C_HEADS, 1, LANES, tr), lambda bb, i: (bb, 0, i // per, 0, i % per))],
        out_shape=[jax.ShapeDtypeStruct((b, lc + s, LANES), BF16),
                   jax.ShapeDtypeStruct((b, nkv, LANES, tk), BF16),
                   jax.ShapeDtypeStruct((b, lc + s, 512), BF16),
                   jax.ShapeDtypeStruct((b, C_HEADS, nkv, LANES, tk), BF16)],
        compiler_params=_cparams(("arbitrary", "arbitrary")),
        name="prep_kv",
    )(pb_l, pb_l, pb_l, pb_l, pb_c, pb_c, pb_c, pb_c, cos_t, sin_t, gk128)


def _pick_tk(kv):
    for tk in (768, 512, 384, 256, 128):
        if kv % tk == 0:
            return tk
    raise ValueError(f"key count {kv} must be a multiple of {LANES}")


def _flash_t(qs_ref, k_ref, vt_fn, tk, nkv):
    rows = qs_ref.shape[0]
    strip = min(QSTRIP, rows)
    ns = rows // strip
    qs = [qs_ref[n * strip:(n + 1) * strip, :] for n in range(ns)]
    nt = (((1,), (1,)), ((), ()))

    def scores(c, n):
        return lax.dot_general(k_ref[0, c * tk:(c + 1) * tk, :], qs[n], nt, preferred_element_type=F32)

    steps = [(c, n) for c in range(nkv) for n in range(ns)]
    ms, ls, accs = [None] * ns, [None] * ns, [None] * ns

    def apply_pv(c, n, p, alpha):
        pv = jnp.dot(vt_fn(c, n), p, preferred_element_type=F32)
        accs[n] = pv if c == 0 else alpha * accs[n] + pv

    pending = None
    st = scores(0, 0)
    for i, (c, n) in enumerate(steps):
        st_next = scores(*steps[i + 1]) if i + 1 < len(steps) else None
        if pending is not None:
            apply_pv(*pending)
        m_cur = jnp.max(st, axis=0, keepdims=True)
        if c == 0:
            m_new, alpha = m_cur, None
            p = jnp.exp2(st - m_new)
            ls[n] = jnp.sum(p, axis=0, keepdims=True)
        else:
            m_new = jnp.maximum(ms[n], m_cur)
            alpha = jnp.exp2(ms[n] - m_new)
            p = jnp.exp2(st - m_new)
            ls[n] = alpha * ls[n] + jnp.sum(p, axis=0, keepdims=True)
        pending = (c, n, p.astype(BF16), alpha)
        ms[n] = m_new
        st = st_next
    apply_pv(*pending)
    return ls, accs


def _attn_a_kernel(q_ref, k_ref, vt_ref, cos_ref, sin_ref, gq_ref, o_ref, qs_ref, *, tq, tk, nkv, use_rope):
    lo = _lane_iota((tq, LANES)) < HEAD_DIM
    for j in range(4):
        x = _rms_heads64(q_ref[0, :, LANES * j:LANES * (j + 1)].astype(F32), gq_ref[...])
        if use_rope:
            x = _rope128(x, cos_ref[...], sin_ref[...])
        x = x * QSCALE
        xr = pltpu.roll(x, HEAD_DIM, 1)
        if j < 2:
            even, odd = jnp.where(lo, x, 0.0), jnp.where(lo, xr, 0.0)
        else:
            even, odd = jnp.where(lo, 0.0, xr), jnp.where(lo, 0.0, x)
        qs_ref[(2 * j) * tq:(2 * j + 1) * tq, :] = even.astype(BF16)
        qs_ref[(2 * j + 1) * tq:(2 * j + 2) * tq, :] = odd.astype(BF16)

    rows = A_HEADS * tq

    def vt_fn(c, n):
        if n * QSTRIP < rows // 2:
            return vt_ref[0, c, 0:HEAD_DIM, :]
        return vt_ref[0, c, HEAD_DIM:, :]

    ls, accs = _flash_t(qs_ref, k_ref, vt_fn, tk, nkv)

    o = jnp.concatenate([acc * (1.0 / l) for acc, l in zip(accs, ls)], axis=1)
    ot = jnp.concatenate([o[:, h * tq:(h + 1) * tq] for h in range(A_HEADS)], axis=0)
    o_ref[0] = ot.T.astype(BF16)


def _attn_a(pb_q, k_all, vt_all, cos_t, sin_t, gq128, kv_rows, use_rope):
    b, s, _ = pb_q.shape
    tq = QSTRIP * 2 // A_HEADS
    tk = min(vt_all.shape[-1], kv_rows)
    nkv = kv_rows // tk
    rows = A_HEADS * tq
    kern = functools.partial(_attn_a_kernel, tq=tq, tk=tk, nkv=nkv, use_rope=use_rope)
    return pl.pallas_call(
        kern,
        grid=(b, s // tq),
        in_specs=[pl.BlockSpec((1, tq, 512), lambda bb, i: (bb, i, P2_QA // 512)),
                  pl.BlockSpec((1, kv_rows, LANES), lambda bb, i: (bb, 0, 0)),
                  pl.BlockSpec((1, nkv, LANES, tk), lambda bb, i: (bb, 0, 0, 0)),
                  pl.BlockSpec((tq, LANES), lambda bb, i: (i, 0)),
                  pl.BlockSpec((tq, LANES), lambda bb, i: (i, 0)),
                  pl.BlockSpec((1, LANES), lambda bb, i: (0, 0))],
        out_specs=pl.BlockSpec((1, tq, 512), lambda bb, i: (bb, i, 0)),
        out_shape=jax.ShapeDtypeStruct((b, s, 512), BF16),
        scratch_shapes=[pltpu.VMEM((rows, LANES), BF16)],
        compiler_params=_cparams(("arbitrary", "arbitrary")),
        name="attn_gqa",
    )(pb_q, k_all, vt_all, cos_t, sin_t, gq128)


def _attn_c_kernel(q_ref, k_ref, vt_ref, cos_ref, sin_ref, dl_ref, sub_ref, o_ref, qs_ref,
                   *, tq, tk, nkv, use_rope, lam_init):
    lo = _lane_iota((tq, LANES)) < HEAD_DIM
    x = q_ref[0].astype(F32)
    if use_rope:
        x = _rope128(x, cos_ref[...], sin_ref[...])
    x = x * QSCALE
    qs_ref[0:tq, :] = jnp.where(lo, x, 0.0).astype(BF16)
    qs_ref[tq:2 * tq, :] = jnp.where(lo, 0.0, x).astype(BF16)

    ls, accs = _flash_t(qs_ref, k_ref, lambda c, n: vt_ref[0, 0, c], tk, nkv)

    dl = dl_ref[...]
    lam = (jnp.exp(jnp.sum(dl[0:1] * dl[1:2], axis=1, keepdims=True))
           - jnp.exp(jnp.sum(dl[2:3] * dl[3:4], axis=1, keepdims=True)) + lam_init)
    o = jnp.concatenate([acc * (1.0 / l) for acc, l in zip(accs, ls)], axis=1)
    y = (o[:, 0:tq] - lam * o[:, tq:2 * tq]).T
    ms = jnp.mean(y * y, axis=1, keepdims=True)
    y = ((y * lax.rsqrt(ms + RMS_EPS)) * sub_ref[...]) * (1.0 - lam_init)
    o_ref[0] = y.astype(BF16)


def _attn_c(pb_q, k_all, vt_all, cos_t, sin_t, dlam, subln, kv_rows, use_rope, lam_init):
    b, s, _ = pb_q.shape
    tq = min(QSTRIP, s)
    tk = min(vt_all.shape[-1], kv_rows)
    nkv = kv_rows // tk
    rows = 2 * tq
    kern = functools.partial(_attn_c_kernel, tq=tq, tk=tk, nkv=nkv, use_rope=use_rope, lam_init=lam_init)
    return pl.pallas_call(
        kern,
        grid=(b, C_HEADS, s // tq),
        in_specs=[pl.BlockSpec((1, tq, LANES), lambda bb, h, i: (bb, i, P2_QC // LANES + h)),
                  pl.BlockSpec((1, kv_rows, LANES), lambda bb, h, i: (bb, 0, h)),
                  pl.BlockSpec((1, 1, nkv, LANES, tk), lambda bb, h, i: (bb, h, 0, 0, 0)),
                  pl.BlockSpec((tq, LANES), lambda bb, h, i: (i, 0)),
                  pl.BlockSpec((tq, LANES), lambda bb, h, i: (i, 0)),
                  pl.BlockSpec((4, HEAD_DIM), lambda bb, h, i: (0, 0)),
                  pl.BlockSpec((1, LANES), lambda bb, h, i: (0, 0))],
        out_specs=pl.BlockSpec((1, tq, LANES), lambda bb, h, i: (bb, i, h)),
        out_shape=jax.ShapeDtypeStruct((b, s, 512), BF16),
        scratch_shapes=[pltpu.VMEM((rows, LANES), BF16)],
        compiler_params=_cparams(("arbitrary", "arbitrary", "arbitrary")),
        name="attn_diff",
    )(pb_q, k_all, vt_all, cos_t, sin_t, dlam, subln)


def _gelu_tanh(x):
    return 0.5 * x * (1.0 + jnp.tanh(math.sqrt(2.0 / math.pi) * (x + 0.044715 * (x * x * x))))


def _lru_kernel(*refs, nb, t, nchunks, reverse, final):
    (x_ref, xp_ref, xn_ref, h0_ref, cw_ref, cb_ref, wa_ref, ba_ref, wx_ref, bx_ref, lam_ref) = refs[:11]
    if final:
        gb_ref, hp_ref, y_ref, hl_ref, a_s, b_s, o_s, h_s = refs[11:]
    else:
        y_ref, hl_ref, a_s, b_s, o_s, h_s = refs[11:]
    c = pl.program_id(0)
    pos = (nchunks - 1 - c) if reverse else c

    @pl.when(c == 0)
    def _():
        h_s[...] = h0_ref[...]

    z = -lam_ref[...]
    softplus = jnp.maximum(z, 0.0) + jnp.log(1.0 + jnp.exp(-jnp.abs(z)))
    cw = cw_ref[...]
    for b in range(nb):
        x = x_ref[b]
        prev = jnp.where(pos == 0, 0.0, xp_ref[b])
        nxt = jnp.where(pos == nchunks - 1, 0.0, xn_ref[b])
        ext = jnp.concatenate([prev, x, nxt], axis=0)
        u = (cw[0:1] * ext[6:6 + t] + cw[1:2] * ext[7:7 + t] + cw[2:3] * x
             + cw[3:4] * ext[9:9 + t]) + cb_ref[...]
        ub = u.astype(BF16)
        r = jax.nn.sigmoid(jnp.dot(ub, wa_ref[...], preferred_element_type=F32) + ba_ref[...])
        gi = jax.nn.sigmoid(jnp.dot(ub, wx_ref[...], preferred_element_type=F32) + bx_ref[...])
        log_a = (-LRU_C * r) * softplus
        a = jnp.exp(log_a)
        mult = jnp.sqrt(-jnp.tanh(log_a) * (a * a + 1.0))
        a_s[b] = a
        b_s[b] = mult * (gi * u)

    ngroups = t // SUBLANES
    sub = lax.broadcasted_iota(I32, (SUBLANES, LRU_WIDTH), 0)
    order = range(SUBLANES - 1, -1, -1) if reverse else range(SUBLANES)

    def group(gidx, hs):
        g = (ngroups - 1 - gidx) if reverse else gidx
        base = pl.multiple_of(g * SUBLANES, SUBLANES)
        new = []
        for b in range(nb):
            ta = a_s[b, pl.ds(base, SUBLANES), :]
            tb = b_s[b, pl.ds(base, SUBLANES), :]
            h = hs[b]
            out = jnp.zeros((SUBLANES, LRU_WIDTH), F32)
            for r in order:
                h = (jnp.broadcast_to(ta[r:r + 1], h.shape) * h
                     + jnp.broadcast_to(tb[r:r + 1], h.shape))
                out = jnp.where(sub == r, h, out)
            o_s[b, pl.ds(base, SUBLANES), :] = out
            new.append(h)
        return tuple(new)

    hs = lax.fori_loop(0, ngroups, group, tuple(h_s[b] for b in range(nb)))
    for b in range(nb):
        h_s[b] = hs[b]
        if final:
            y = hp_ref[b] + o_s[b]
            y_ref[b] = (y * _gelu_tanh(gb_ref[b].astype(F32))).astype(y_ref.dtype)
        else:
            y_ref[b] = o_s[b]
    hl_ref[...] = h_s[...]


def _lru_pass(p1, h0, prm, d, reverse, gb_src=None, hprev=None):
    nb, length, _ = p1.shape
    t = min(256, length)
    nchunks = length // t
    final = gb_src is not None
    blk8 = t // SUBLANES
    nblk8 = length // SUBLANES
    xcol = P1_XB // LRU_WIDTH
    cpos = (lambda c: nchunks - 1 - c) if reverse else (lambda c: c)
    vec = lambda: pl.BlockSpec((1, LRU_WIDTH), lambda c: (0, 0))
    mat = lambda: pl.BlockSpec((LRU_WIDTH, LRU_WIDTH), lambda c: (0, 0))
    in_specs = [pl.BlockSpec((nb, t, LRU_WIDTH), lambda c: (0, cpos(c), xcol)),
                pl.BlockSpec((nb, SUBLANES, LRU_WIDTH),
                             lambda c: (0, jnp.maximum(cpos(c) * blk8 - 1, 0), xcol)),
                pl.BlockSpec((nb, SUBLANES, LRU_WIDTH),
                             lambda c: (0, jnp.minimum((cpos(c) + 1) * blk8, nblk8 - 1), xcol)),
                pl.BlockSpec((nb, SUBLANES, LRU_WIDTH), lambda c: (0, 0, 0)),
                pl.BlockSpec((CONV_WIDTH, LRU_WIDTH), lambda c: (0, 0)), vec(),
                mat(), vec(), mat(), vec(), vec()]
    args = [p1, p1, p1, h0, prm["conv_w"], prm["conv_b"], prm["wa"][d], prm["ba"][d],
            prm["wx"][d], prm["bx"][d], prm["lam"][d]]
    if final:
        in_specs += [pl.BlockSpec((nb, t, LRU_WIDTH), lambda c: (0, cpos(c), P2_GB // LRU_WIDTH)),
                     pl.BlockSpec((nb, t, LRU_WIDTH), lambda c: (0, cpos(c), 0))]
        args += [gb_src, hprev]
    kern = functools.partial(_lru_kernel, nb=nb, t=t, nchunks=nchunks, reverse=reverse, final=final)
    return pl.pallas_call(
        kern,
        grid=(nchunks,),
        in_specs=in_specs,
        out_specs=[pl.BlockSpec((nb, t, LRU_WIDTH), lambda c: (0, cpos(c), 0)),
                   pl.BlockSpec((nb, SUBLANES, LRU_WIDTH), lambda c: (0, 0, 0))],
        out_shape=[jax.ShapeDtypeStruct((nb, length, LRU_WIDTH), BF16 if final else F32),
                   jax.ShapeDtypeStruct((nb, SUBLANES, LRU_WIDTH), F32)],
        scratch_shapes=[pltpu.VMEM((nb, t, LRU_WIDTH), F32), pltpu.VMEM((nb, t, LRU_WIDTH), F32),
                        pltpu.VMEM((nb, t, LRU_WIDTH), F32), pltpu.VMEM((nb, SUBLANES, LRU_WIDTH), F32)],
        compiler_params=_cparams(("arbitrary",)),
        name="lru_bwd" if reverse else "lru_fwd",
    )(*args)


def _merge_kernel(ya_ref, yb_ref, yc_ref, gm_ref, x_ref, g1_ref, sc2_ref, sh2_ref, lg_ref, lb_ref,
                  wb_ref, wo_ref, wr_ref, x1_ref, h2_ref, aff_ref):
    acc = None
    for n, br in enumerate((ya_ref, yb_ref, yc_ref)):
        p = jnp.dot(br[0], wb_ref[n], preferred_element_type=F32)
        gate = jax.nn.sigmoid(gm_ref[0, :, n * D_MODEL:(n + 1) * D_MODEL])
        acc = gate * p if acc is None else acc + gate * p
    o = jnp.dot(acc.astype(BF16), wo_ref[...], preferred_element_type=F32)
    x1 = _layer_norm(DN_ALPHA * x_ref[0] + g1_ref[0] * o, lg_ref[...], lb_ref[...])
    x1_ref[0] = x1
    h2 = x1 * (1.0 + sc2_ref[0]) + sh2_ref[0]
    h2_ref[0] = h2
    logits = jnp.dot(h2.astype(BF16), wr_ref[...], preferred_element_type=F32)
    logits = jnp.where(_lane_iota(logits.shape) < N_EXPERTS, logits, -jnp.inf)
    e = jnp.exp(logits - jnp.max(logits, axis=-1, keepdims=True))
    aff_ref[0] = e / jnp.sum(e, axis=-1, keepdims=True)


def _merge(ya, yb, yc, p1, x, g1, sc2, sh2, lg, lb, wb, wo, wr):
    b, s, d = x.shape
    tm = min(512, s)
    row = lambda w, col=0: pl.BlockSpec((1, tm, w), lambda bb, i: (bb, i, col))
    per_b = lambda: pl.BlockSpec((1, 1, d), lambda bb, i: (bb, 0, 0))
    full = lambda shape: pl.BlockSpec(shape, lambda bb, i: (0,) * len(shape))
    return pl.pallas_call(
        _merge_kernel,
        grid=(b, s // tm),
        in_specs=[row(512), row(512), row(512), row(N_BRANCH * d, P1_GM // (N_BRANCH * d)), row(d),
                  per_b(), per_b(), per_b(), full((1, d)), full((1, d)),
                  full((N_BRANCH, BRANCH_W, d)), full((d, d)), full((d, LANES))],
        out_specs=[row(d), row(d), row(LANES)],
        out_shape=[jax.ShapeDtypeStruct((b, s, d), F32), jax.ShapeDtypeStruct((b, s, d), F32),
                   jax.ShapeDtypeStruct((b, s, LANES), F32)],
        compiler_params=_cparams(("arbitrary", "arbitrary")),
        name="merge_ln1_router",
    )(ya, yb, yc, p1, x, g1, sc2, sh2, lg, lb, wb, wo, wr)


def _route_kernel(aff_ref, pos_ref, base_ref, cnt_ref, x_s, c_s, *, n, cap, blk):
    v = aff_ref[0]
    kf = float(cap)

    def bit_body(i, thr):
        cand = thr | lax.shift_left(jnp.int32(1), 30 - i)
        hit = jnp.where(v >= pltpu.bitcast(cand, F32), 1.0, 0.0)
        return jnp.where(jnp.sum(hit, axis=0, keepdims=True) >= kf, cand, thr)

    thr = pltpu.bitcast(lax.fori_loop(0, 31, bit_body, jnp.zeros((1, LANES), I32)), F32)
    gt = v > thr
    eq = v == thr
    need = kf - jnp.sum(jnp.where(gt, 1.0, 0.0), axis=0, keepdims=True)

    nblk = n // blk
    tri = (lax.broadcasted_iota(I32, (blk, blk), 0) > lax.broadcasted_iota(I32, (blk, blk), 1))
    tri = jnp.where(tri, 1.0, 0.0).astype(BF16)

    def excl_cumsum(write_base):
        carry = jnp.zeros((1, LANES), F32)
        for j in range(nblk):
            xb = x_s[j * blk:(j + 1) * blk, :]
            c_s[j * blk:(j + 1) * blk, :] = (
                jnp.dot(tri, xb.astype(BF16), preferred_element_type=F32) + carry)
            if write_base:
                base_ref[0, j:j + 1, :] = carry.astype(I32)
            carry = carry + jnp.sum(xb, axis=0, keepdims=True)

    x_s[...] = jnp.where(eq, 1.0, 0.0)
    excl_cumsum(False)
    take_eq = jnp.where(c_s[...] < need, 1.0, 0.0)
    sel = jnp.where(gt, 1.0, jnp.where(eq, take_eq, 0.0))
    sel = jnp.where(_lane_iota(sel.shape) < N_EXPERTS, sel, 0.0)
    x_s[...] = sel
    excl_cumsum(True)
    pos_ref[0] = jnp.where(x_s[...] > 0.5, c_s[...].astype(I32), -1)
    cnt_ref[0] = (c_s[...] + x_s[...]).astype(I32)


def _route(aff, cap, blk):
    b, n, _ = aff.shape
    kern = functools.partial(_route_kernel, n=n, cap=cap, blk=blk)
    tok = lambda: pl.BlockSpec((1, n, LANES), lambda bb: (bb, 0, 0))
    return pl.pallas_call(
        kern,
        grid=(b,),
        in_specs=[tok()],
        out_specs=[tok(), pl.BlockSpec((1, n // blk, LANES), lambda bb: (bb, 0, 0)), tok()],
        out_shape=[jax.ShapeDtypeStruct((b, n, LANES), I32),
                   jax.ShapeDtypeStruct((b, n // blk, LANES), I32),
                   jax.ShapeDtypeStruct((b, n, LANES), I32)],
        scratch_shapes=[pltpu.VMEM((n, LANES), F32), pltpu.VMEM((n, LANES), F32)],
        compiler_params=_cparams(("arbitrary",)),
        name="route",
    )(aff)


def _slot_index_kernel(base_ref, cnt_ref, idx_ref, acc_s, add_s, *, nblk, cap):
    b = pl.program_id(0)
    ncol = max(cap // LANES, 1)
    pw = min(cap, LANES)
    lane = _lane_iota((LANES, LANES))
    for e in range(N_EXPERTS):
        acc_s[...] = jnp.zeros(acc_s.shape, F32)
        add_s[...] = jnp.zeros(add_s.shape, F32)

        def block(r, carry, e=e):
            k0 = lax.shift_right_logical(base_ref[(b * nblk + r) * N_EXPERTS + e], 7)
            off = pl.multiple_of(r * LANES, LANES)
            cb = jnp.broadcast_to(cnt_ref[0, pl.ds(off, LANES), e:e + 1], (LANES, LANES))
            for dk in range(2):
                hit = jnp.where(cb <= lane + (k0 + dk) * LANES, 1.0, 0.0)
                acc_s[k0 + dk] += jnp.sum(hit.reshape(-1, SUBLANES, LANES), axis=0)
            add_s[k0 + 2] += float(LANES // SUBLANES)
            return carry

        lax.fori_loop(0, nblk, block, 0, unroll=min(8, nblk))
        run = jnp.zeros((SUBLANES, LANES), F32)
        for k in range(ncol):
            run = run + add_s[k]
            tot = jnp.sum(acc_s[k] + run, axis=0, keepdims=True)
            idx_ref[0, e:e + 1, LANES * k:LANES * k + pw] = tot[:, :pw].astype(I32)


def _slot_index(base_flat, cnt, cap):
    b, n, _ = cnt.shape
    nblk = n // LANES
    ncol = max(cap // LANES, 1)
    kern = functools.partial(_slot_index_kernel, nblk=nblk, cap=cap)
    grid_spec = pltpu.PrefetchScalarGridSpec(
        num_scalar_prefetch=1,
        grid=(b,),
        in_specs=[pl.BlockSpec((1, n, LANES), lambda bb, base: (bb, 0, 0))],
        out_specs=pl.BlockSpec((1, N_EXPERTS, cap), lambda bb, base: (bb, 0, 0)),
        scratch_shapes=[pltpu.VMEM((ncol + 2, SUBLANES, LANES), F32),
                        pltpu.VMEM((ncol + 3, SUBLANES, LANES), F32)],
    )
    return pl.pallas_call(
        kern,
        grid_spec=grid_spec,
        out_shape=jax.ShapeDtypeStruct((b, N_EXPERTS, cap), I32),
        compiler_params=_cparams(("arbitrary",)),
        name="slot_index",
    )(base_flat, cnt)


def _expert_kernel(idx_ref, h2_hbm, wg_ref, wu_ref, wd_ref, y_ref, xbuf, sem, *, nb, ne, cap, tsub):
    e = pl.program_id(0)
    b = pl.program_id(1)
    step = e * nb + b
    slot = step % 2

    def gather(st, sl, r0, r1):
        bb = st % nb
        off = (bb * ne + st // nb) * cap

        def body(r, carry):
            tok = idx_ref[off + r]
            pltpu.make_async_copy(h2_hbm.at[bb, pl.ds(tok, 1), :], xbuf.at[sl, pl.ds(r, 1), :],
                                  sem.at[sl]).start()
            return carry

        lax.fori_loop(r0, r1, body, 0, unroll=8)

    @pl.when(step == 0)
    def _():
        gather(step, slot, 0, cap)

    pltpu.make_async_copy(xbuf.at[slot], xbuf.at[slot], sem.at[slot]).wait()

    for s in range(cap // tsub):
        @pl.when(step + 1 < nb * ne)
        def _():
            gather(step + 1, 1 - slot, s * tsub, (s + 1) * tsub)

        x = xbuf[slot, s * tsub:(s + 1) * tsub, :].astype(BF16)
        g = jnp.dot(x, wg_ref[0, 0], preferred_element_type=F32)
        u = jnp.dot(x, wu_ref[0, 0], preferred_element_type=F32)
        act = ((g * jax.nn.sigmoid(g)) * u).astype(BF16)
        y_ref[0, 0, s * tsub:(s + 1) * tsub, :] = jnp.dot(
            act, wd_ref[0, 0], preferred_element_type=F32).astype(y_ref.dtype)


def _experts(idx_flat, h2, wg, wu, wd, l, cap):
    nb, n, d = h2.shape
    _, ne, _, fp = wg.shape
    tsub = min(256, cap)
    kern = functools.partial(_expert_kernel, nb=nb, ne=ne, cap=cap, tsub=tsub)
    grid_spec = pltpu.PrefetchScalarGridSpec(
        num_scalar_prefetch=1,
        grid=(ne, nb),
        in_specs=[pl.BlockSpec(memory_space=pl.ANY),
                  pl.BlockSpec((1, 1, d, fp), lambda e, b, idx: (l, e, 0, 0)),
                  pl.BlockSpec((1, 1, d, fp), lambda e, b, idx: (l, e, 0, 0)),
                  pl.BlockSpec((1, 1, fp, d), lambda e, b, idx: (l, e, 0, 0))],
        out_specs=pl.BlockSpec((1, 1, cap, d), lambda e, b, idx: (b, e, 0, 0)),
        scratch_shapes=[pltpu.VMEM((2, cap, d), F32), pltpu.SemaphoreType.DMA((2,))],
    )
    return pl.pallas_call(
        kern,
        grid_spec=grid_spec,
        out_shape=jax.ShapeDtypeStruct((nb, ne, cap, d), BF16),
        compiler_params=_cparams(("arbitrary", "arbitrary"), 60 * 1024 * 1024),
        name="expert_ffn",
    )(idx_flat, h2, wg, wu, wd)


def _combine_kernel(base_ref, x1_ref, pos_ref, aff_ref, g2_ref, lg_ref, lb_ref, y_hbm, o_ref, ybuf, sem,
                    *, nb, nt, ne, cap, tt, win):
    b = pl.program_id(0)
    i = pl.program_id(1)
    step = b * nt + i
    slot = step % 2

    def win_start(st, e):
        base = base_ref[st * ne + e]
        al = lax.shift_left(lax.shift_right_logical(base, 4), 4)
        return pl.multiple_of(jnp.minimum(al, cap - win), BF16_ROWS)

    def fetch(st, sl):
        bb = st // nt
        for e in range(ne):
            pltpu.make_async_copy(y_hbm.at[bb, e, pl.ds(win_start(st, e), win), :], ybuf.at[sl, e],
                                  sem.at[sl]).start()

    @pl.when(step == 0)
    def _():
        fetch(step, slot)

    @pl.when(step + 1 < nb * nt)
    def _():
        fetch(step + 1, 1 - slot)

    for e in range(ne):
        pltpu.make_async_copy(y_hbm.at[0, e, pl.ds(0, win), :], ybuf.at[slot, e], sem.at[slot]).wait()

    pos = pos_ref[0]
    aff = aff_ref[0]
    col = _lane_iota((tt, win))
    acc = jnp.zeros((tt, D_MODEL), F32)
    for e in range(ne):
        rel = pos[:, e:e + 1] - win_start(step, e)
        onehot = jnp.where(col == rel, 1.0, 0.0).astype(BF16)
        acc = acc + aff[:, e:e + 1] * jnp.dot(onehot, ybuf[slot, e], preferred_element_type=F32)
    o_ref[0] = _layer_norm(DN_ALPHA * x1_ref[0] + g2_ref[0] * acc, lg_ref[...], lb_ref[...])


def _combine(base_flat, x1, pos, aff, g2, lg, lb, y, tt):
    nb, n, d = x1.shape
    ne, cap = y.shape[1], y.shape[2]
    nt = n // tt
    win = min(tt + BF16_ROWS, cap)
    kern = functools.partial(_combine_kernel, nb=nb, nt=nt, ne=ne, cap=cap, tt=tt, win=win)
    row = lambda w: pl.BlockSpec((1, tt, w), lambda b, i, base: (b, i, 0))
    grid_spec = pltpu.PrefetchScalarGridSpec(
        num_scalar_prefetch=1,
        grid=(nb, nt),
        in_specs=[row(d), row(LANES), row(LANES),
                  pl.BlockSpec((1, 1, d), lambda b, i, base: (b, 0, 0)),
                  pl.BlockSpec((1, d), lambda b, i, base: (0, 0)),
                  pl.BlockSpec((1, d), lambda b, i, base: (0, 0)),
                  pl.BlockSpec(memory_space=pl.ANY)],
        out_specs=row(d),
        scratch_shapes=[pltpu.VMEM((2, ne, win, d), BF16), pltpu.SemaphoreType.DMA((2,))],
    )
    return pl.pallas_call(
        kern,
        grid_spec=grid_spec,
        out_shape=jax.ShapeDtypeStruct((nb, n, d), F32),
        compiler_params=_cparams(("arbitrary", "arbitrary")),
        name="combine_ln2",
    )(base_flat, x1, pos, aff, g2, lg, lb, y)


def _moe(x1, h2, aff, g2, lg, lb, wg, wu, wd, l):
    nb, n, d = x1.shape
    cap = EC_FACTOR * n // N_EXPERTS
    tt = min(128, n)
    assert tt == min(LANES, n)
    pos, base, cnt = _route(aff, cap, tt)
    base_flat = base[:, :, :N_EXPERTS].reshape(-1)
    idx = _slot_index(base_flat, cnt, cap)
    y = _experts(idx.reshape(-1), h2, wg, wu, wd, l, cap)
    return _combine(base_flat, x1, pos, aff, g2, lg, lb, y, tt)


def _rope_tables(seq):
    rows = seq // GRID_W
    row = jnp.repeat(jnp.arange(rows, dtype=I32), GRID_W)
    colm = jnp.tile(jnp.arange(GRID_W, dtype=I32), rows)
    inv = ROPE_THETA ** (-jnp.arange(N_FREQ, dtype=F32) / N_FREQ)
    ang = jnp.stack([row, colm], -1).astype(F32)[..., None] * inv
    cos, sin = jnp.cos(ang), jnp.sin(ang)
    c64 = jnp.concatenate([cos[:, 0], cos[:, 0], cos[:, 1], cos[:, 1]], axis=-1)
    s64 = jnp.concatenate([-sin[:, 0], sin[:, 0], -sin[:, 1], sin[:, 1]], axis=-1)
    return jnp.tile(c64, (1, 2)), jnp.tile(s64, (1, 2))


def _block_diag(w):
    nblk, bw, _ = w.shape
    eye = jnp.eye(nblk, dtype=w.dtype)
    return (eye[:, None, :, None] * w[:, :, None, :]).reshape(nblk * bw, nblk * bw)


def kernel(x, c, ctx, c_ctx, w_ada, b_ada, w_in, a_q_norm, a_k_norm, lru_conv_w, lru_conv_b, lru_w_a,
           lru_b_a, lru_w_x, lru_b_x, lru_lambda, diff_lambda, diff_subln, w_branch, w_out, ln1_g, ln1_b,
           w_router, w_gate, w_up, w_down, ln2_g, ln2_b):
    nb, seq, d = x.shape
    lc = ctx.shape[1]
    depth = w_in.shape[0]
    assert d == D_MODEL and nb + 1 <= SUBLANES and seq % lc == 0 and lc % LANES == 0

    cos_t, sin_t = _rope_tables(seq)
    cc = jnp.zeros((SUBLANES, d), F32).at[:nb].set(c).at[nb].set(c_ctx)
    mod_all = _ada(cc, w_ada, b_ada)

    zeros8 = jnp.zeros((nb, SUBLANES, LRU_WIDTH), F32)

    w1 = jnp.concatenate([w_in[:, :, 3328:6400], w_in[:, :, 768:1280]], axis=2).astype(BF16)
    w2 = jnp.concatenate([w_in[:, :, 0:512], w_in[:, :, 1280:1792], w_in[:, :, 1792:2304],
                          w_in[:, :, 2304:2816], w_in[:, :, 2816:3328], w_in[:, :, 512:640],
                          w_in[:, :, 640:768]], axis=2).astype(BF16)
    wg, wu, wd = w_gate.astype(BF16), w_up.astype(BF16), w_down.astype(BF16)

    for l in range(depth):
        need_ctx = l < depth - 1
        lam_init = 0.8 - 0.6 * math.exp(-0.3 * l)
        mod = mod_all[l]
        m6 = mod[:nb].reshape(nb, 6, 1, d)
        sh1, sc1, g1, sh2, sc2, g2 = (m6[:, k] for k in range(6))
        mc6 = jnp.broadcast_to(mod[nb].reshape(1, 6, 1, d), (nb, 6, 1, d))
        sh1c, sc1c, g1c, sh2c, sc2c, g2c = (mc6[:, k] for k in range(6))

        gq128 = jnp.tile(a_q_norm[l], 2).reshape(1, LANES)
        gk128 = jnp.tile(a_k_norm[l], 2).reshape(1, LANES)
        lru = {
            "conv_w": lru_conv_w[l], "conv_b": lru_conv_b[l].reshape(1, -1),
            "wa": [_block_diag(lru_w_a[l, k]).astype(BF16) for k in range(2)],
            "wx": [_block_diag(lru_w_x[l, k]).astype(BF16) for k in range(2)],
            "ba": [lru_b_a[l, k].reshape(1, -1) for k in range(2)],
            "bx": [lru_b_x[l, k].reshape(1, -1) for k in range(2)],
            "lam": [lru_lambda[l, k].reshape(1, -1) for k in range(2)],
        }
        subln = diff_subln[l].reshape(1, LANES)
        wb = w_branch[l].astype(BF16)
        wo = w_out[l].astype(BF16)
        wr = jnp.zeros((d, LANES), F32).at[:, :N_EXPERTS].set(w_router[l]).astype(BF16)
        lg1, lb1 = ln1_g[l].reshape(1, d), ln1_b[l].reshape(1, d)
        lg2, lb2 = ln2_g[l].reshape(1, d), ln2_b[l].reshape(1, d)

        p1 = _mod_matmul(x, sc1, sh1, w1, l, F32, P1_W, "inproj_f32")
        p2 = _mod_matmul(x, sc1, sh1, w2, l, BF16, P2_W, "inproj_bf16")
        p1c = _mod_matmul(ctx, sc1c, sh1c, w1, l, F32, P1_W, "inproj_f32")
        p2c = _mod_matmul(ctx, sc1c, sh1c, w2, l, BF16, P2_W, "inproj_bf16")

        kv = lc + seq
        ka, vat, kc, vct = _prep_kv(p2, p2c, cos_t, sin_t, gk128, _pick_tk(kv))
        ya = _attn_a(p2, ka, vat, cos_t, sin_t, gq128, kv, True)
        yc = _attn_c(p2, kc, vct, cos_t, sin_t, diff_lambda[l], subln, kv, True, lam_init)

        hf_c, hl_f = _lru_pass(p1c, zeros8, lru, 0, False)
        yb_c, hl_b = _lru_pass(p1c, zeros8, lru, 1, True, gb_src=p2c, hprev=hf_c)
        hf, _ = _lru_pass(p1, hl_f, lru, 0, False)
        yb, _ = _lru_pass(p1, hl_b, lru, 1, True, gb_src=p2, hprev=hf)

        x1, h2, aff = _merge(ya, yb, yc, p1, x, g1, sc2, sh2, lg1, lb1, wb, wo, wr)
        x_next = _moe(x1, h2, aff, g2, lg2, lb2, wg, wu, wd, l)

        if need_ctx:
            ya_c = _attn_a(p2c, ka, vat, cos_t, sin_t, gq128, lc, False)
            yc_c = _attn_c(p2c, kc, vct, cos_t, sin_t, diff_lambda[l], subln, lc, False, lam_init)
            c1, hc2, affc = _merge(ya_c, yb_c, yc_c, p1c, ctx, g1c, sc2c, sh2c, lg1, lb1, wb, wo, wr)
            ctx = _moe(c1, hc2, affc, g2c, lg2, lb2, wg, wu, wd, l)
        x = x_next
    return x
```

```python
import functools
import math

import jax
import jax.numpy as jnp
from jax import lax
from jax.experimental import pallas as pl
from jax.experimental.pallas import tpu as pltpu

F32 = jnp.float32
BF16 = jnp.bfloat16
I32 = jnp.int32

D_MODEL = 1024
DEPTH = 2
GRID_W = 64
HEAD_DIM = 64
N_FREQ = HEAD_DIM // 4
ROPE_THETA = 10000.0
A_HEADS = 8
A_KV_HEADS = 2
LRU_WIDTH = 512
LRU_BLOCKS = 8
CONV_WIDTH = 4
LRU_C = 8.0
C_HEADS = 4
N_BRANCH = 3
BRANCH_W = 512
N_EXPERTS = 16
EC_FACTOR = 2
DN_ALPHA = (2 * DEPTH) ** 0.25
LN_EPS = 1e-5
RMS_EPS = 1e-6

LANES = 128
SUBLANES = 8
BF16_ROWS = 16
MXU_DIM = 256
QSTRIP = 2 * MXU_DIM
VMEM_LIMIT = 56 * 1024 * 1024

LOG2E = 1.4426950408889634
QSCALE = (HEAD_DIM ** -0.5) * LOG2E

P1_GM = 0
P1_XB = N_BRANCH * D_MODEL
P1_W = P1_XB + LRU_WIDTH
P2_QA, P2_GB, P2_QC, P2_KC, P2_VC, P2_KA, P2_VA = 0, 512, 1024, 1536, 2048, 2560, 2688
P2_W = 2816


def _cparams(sem, vmem=VMEM_LIMIT):
    return pltpu.CompilerParams(dimension_semantics=sem, vmem_limit_bytes=vmem)


def _lane_iota(shape):
    return lax.broadcasted_iota(I32, shape, len(shape) - 1)


def _layer_norm(z, g, b):
    mu = jnp.mean(z, axis=-1, keepdims=True)
    zc = z - mu
    var = jnp.mean(zc * zc, axis=-1, keepdims=True)
    return (zc * lax.rsqrt(var + LN_EPS)) * g + b


def _rms_heads64(x, g128):
    lo = _lane_iota(x.shape) < HEAD_DIM
    x2 = x * x
    s_lo = jnp.sum(jnp.where(lo, x2, 0.0), axis=-1, keepdims=True)
    s_hi = jnp.sum(jnp.where(lo, 0.0, x2), axis=-1, keepdims=True)
    inv = jnp.where(lo, lax.rsqrt(s_lo * (1.0 / HEAD_DIM) + RMS_EPS),
                    lax.rsqrt(s_hi * (1.0 / HEAD_DIM) + RMS_EPS))
    return (x * inv) * g128


def _rope128(x, c, s):
    first = (_lane_iota(x.shape) & (2 * N_FREQ - 1)) < N_FREQ
    partner = jnp.where(first, pltpu.roll(x, LANES - N_FREQ, 1), pltpu.roll(x, N_FREQ, 1))
    return x * c + partner * s


def _ada_kernel(c_ref, w_ref, b_ref, o_ref):
    c = c_ref[...]
    s = c * jax.nn.sigmoid(c)
    o_ref[0] = jnp.dot(s.astype(BF16), w_ref[0].astype(BF16), preferred_element_type=F32) + b_ref[0]


def _ada(cc, w_ada, b_ada):
    nl, d, n6 = w_ada.shape
    tn = 1536
    return pl.pallas_call(
        _ada_kernel,
        grid=(nl, n6 // tn),
        in_specs=[pl.BlockSpec((SUBLANES, d), lambda l, j: (0, 0)),
                  pl.BlockSpec((1, d, tn), lambda l, j: (l, 0, j)),
                  pl.BlockSpec((1, 1, tn), lambda l, j: (l, 0, j))],
        out_specs=pl.BlockSpec((1, SUBLANES, tn), lambda l, j: (l, 0, j)),
        out_shape=jax.ShapeDtypeStruct((nl, SUBLANES, n6), F32),
        compiler_params=_cparams(("arbitrary", "arbitrary")),
        name="ada",
    )(cc, w_ada, b_ada.reshape(nl, 1, n6))


def _mod_matmul_kernel(x_ref, sc_ref, sh_ref, w_ref, o_ref):
    h = x_ref[0] * (1.0 + sc_ref[0]) + sh_ref[0]
    o_ref[0] = jnp.dot(h.astype(BF16), w_ref[0], preferred_element_type=F32).astype(o_ref.dtype)


def _mod_matmul(x, sc, sh, w, l, out_dtype, tn, name):
    g, r, k = x.shape
    n = w.shape[2]
    tm = min(512, r)
    return pl.pallas_call(
        _mod_matmul_kernel,
        grid=(g, r // tm, n // tn),
        in_specs=[pl.BlockSpec((1, tm, k), lambda a, i, j: (a, i, 0)),
                  pl.BlockSpec((1, 1, k), lambda a, i, j: (a, 0, 0)),
                  pl.BlockSpec((1, 1, k), lambda a, i, j: (a, 0, 0)),
                  pl.BlockSpec((1, k, tn), lambda a, i, j: (l, 0, j))],
        out_specs=pl.BlockSpec((1, tm, tn), lambda a, i, j: (a, i, j)),
        out_shape=jax.ShapeDtypeStruct((g, r, n), out_dtype),
        compiler_params=_cparams(("arbitrary", "arbitrary", "arbitrary")),
        name=name,
    )(x, sc, sh, w)


def _prep_kv_kernel(kc_l, vc_l, ka_l, va_l, kc_c, vc_c, ka_c, va_c, cos_ref, sin_ref, gk_ref,
                    ka_o, vat_o, kc_o, vct_o):
    i = pl.program_id(1)

    def put_values(va, vc):
        vat_o[0, 0] = va.astype(F32).T.astype(BF16)
        for h in range(C_HEADS):
            vct_o[0, h, 0] = vc[:, LANES * h:LANES * (h + 1)].astype(F32).T.astype(BF16)

    @pl.when(i == 0)
    def _():
        ka_o[0] = _rms_heads64(ka_c[0].astype(F32), gk_ref[...]).astype(BF16)
        kc_o[0] = kc_c[0]
        put_values(va_c[0], vc_c[0])

    @pl.when(i > 0)
    def _():
        c, s = cos_ref[...], sin_ref[...]
        ka = _rms_heads64(ka_l[0].astype(F32), gk_ref[...])
        ka_o[0] = _rope128(ka, c, s).astype(BF16)
        for j in range(4):
            sl = slice(LANES * j, LANES * (j + 1))
            kc_o[0, :, sl] = _rope128(kc_l[0, :, sl].astype(F32), c, s).astype(BF16)
        put_values(va_l[0], vc_l[0])


def _prep_kv(pb_l, pb_c, cos_t, sin_t, gk128, tk):
    b, s, _ = pb_l.shape
    lc = pb_c.shape[1]
    tr = lc
    nb = (lc + s) // tr
    per = tk // tr
    nkv = (lc + s) // tk
    li = lambda i: jnp.maximum(i - 1, 0)
    lat = lambda col: (lambda bb, i: (bb, li(i), col))
    ctx = lambda col: (lambda bb, i: (bb, 0, col))
    wide, nar = (1, tr, 512), (1, tr, LANES)
    outw = lambda w: pl.BlockSpec((1, tr, w), lambda bb, i: (bb, i, 0))
    return pl.pallas_call(
        _prep_kv_kernel,
        grid=(b, nb),
        in_specs=[pl.BlockSpec(wide, lat(P2_KC // 512)), pl.BlockSpec(wide, lat(P2_VC // 512)),
                  pl.BlockSpec(nar, lat(P2_KA // LANES)), pl.BlockSpec(nar, lat(P2_VA // LANES)),
                  pl.BlockSpec(wide, ctx(P2_KC // 512)), pl.BlockSpec(wide, ctx(P2_VC // 512)),
                  pl.BlockSpec(nar, ctx(P2_KA // LANES)), pl.BlockSpec(nar, ctx(P2_VA // LANES)),
                  pl.BlockSpec((tr, LANES), lambda bb, i: (li(i), 0)),
                  pl.BlockSpec((tr, LANES), lambda bb, i: (li(i), 0)),
                  pl.BlockSpec((1, LANES), lambda bb, i: (0, 0))],
        out_specs=[outw(LANES),
                   pl.BlockSpec((1, 1, LANES, tr), lambda bb, i: (bb, i // per, 0, i % per)),
                   outw(512),
                   pl.BlockSpec((1, C_HEADS, 1, LANES, tr), lambda bb, i: (bb, 0, i // per, 0, i % per))],
        out_shape=[jax.ShapeDtypeStruct((b, lc + s, LANES), BF16),
                   jax.ShapeDtypeStruct((b, nkv, LANES, tk), BF16),
                   jax.ShapeDtypeStruct((b, lc + s, 512), BF16),
                   jax.ShapeDtypeStruct((b, C_HEADS, nkv, LANES, tk), BF16)],
        compiler_params=_cparams(("arbitrary", "arbitrary")),
        name="prep_kv",
    )(pb_l, pb_l, pb_l, pb_l, pb_c, pb_c, pb_c, pb_c, cos_t, sin_t, gk128)


def _pick_tk(kv):
    for tk in (768, 512, 384, 256, 128):
        if kv % tk == 0:
            return tk
    raise ValueError(f"key count {kv} must be a multiple of {LANES}")


def _flash_t(qs_ref, k_ref, vt_fn, tk, nkv):
    rows = qs_ref.shape[0]
    strip = min(QSTRIP, rows)
    ns = rows // strip
    qs = [qs_ref[n * strip:(n + 1) * strip, :] for n in range(ns)]
    nt = (((1,), (1,)), ((), ()))

    def scores(c, n):
        return lax.dot_general(k_ref[0, c * tk:(c + 1) * tk, :], qs[n], nt, preferred_element_type=F32)

    steps = [(c, n) for c in range(nkv) for n in range(ns)]
    ms, ls, accs = [None] * ns, [None] * ns, [None] * ns

    def apply_pv(c, n, p, alpha):
        pv = jnp.dot(vt_fn(c, n), p, preferred_element_type=F32)
        accs[n] = pv if c == 0 else alpha * accs[n] + pv

    pending = None
    st = scores(0, 0)
    for i, (c, n) in enumerate(steps):
        st_next = scores(*steps[i + 1]) if i + 1 < len(steps) else None
        if pending is not None:
            apply_pv(*pending)
        m_cur = jnp.max(st, axis=0, keepdims=True)
        if c == 0:
            m_new, alpha = m_cur, None
            p = jnp.exp2(st - m_new)
            ls[n] = jnp.sum(p, axis=0, keepdims=True)
        else:
            m_new = jnp.maximum(ms[n], m_cur)
            alpha = jnp.exp2(ms[n] - m_new)
            p = jnp.exp2(st - m_new)
            ls[n] = alpha * ls[n] + jnp.sum(p, axis=0, keepdims=True)
        pending = (c, n, p.astype(BF16), alpha)
        ms[n] = m_new
        st = st_next
    apply_pv(*pending)
    return ls, accs


def _attn_a_kernel(q_ref, k_ref, vt_ref, cos_ref, sin_ref, gq_ref, o_ref, qs_ref, *, tq, tk, nkv, use_rope):
    lo = _lane_iota((tq, LANES)) < HEAD_DIM
    for j in range(4):
        x = _rms_heads64(q_ref[0, :, LANES * j:LANES * (j + 1)].astype(F32), gq_ref[...])
        if use_rope:
            x = _rope128(x, cos_ref[...], sin_ref[...])
        x = x * QSCALE
        xr = pltpu.roll(x, HEAD_DIM, 1)
        if j < 2:
            even, odd = jnp.where(lo, x, 0.0), jnp.where(lo, xr, 0.0)
        else:
            even, odd = jnp.where(lo, 0.0, xr), jnp.where(lo, 0.0, x)
        qs_ref[(2 * j) * tq:(2 * j + 1) * tq, :] = even.astype(BF16)
        qs_ref[(2 * j + 1) * tq:(2 * j + 2) * tq, :] = odd.astype(BF16)

    rows = A_HEADS * tq

    def vt_fn(c, n):
        if n * QSTRIP < rows // 2:
            return vt_ref[0, c, 0:HEAD_DIM, :]
        return vt_ref[0, c, HEAD_DIM:, :]

    ls, accs = _flash_t(qs_ref, k_ref, vt_fn, tk, nkv)

    o = jnp.concatenate([acc * (1.0 / l) for acc, l in zip(accs, ls)], axis=1)
    ot = jnp.concatenate([o[:, h * tq:(h + 1) * tq] for h in range(A_HEADS)], axis=0)
    o_ref[0] = ot.T.astype(BF16)


def _attn_a(pb_q, k_all, vt_all, cos_t, sin_t, gq128, kv_rows, use_rope):
    b, s, _ = pb_q.shape
    tq = QSTRIP * 2 // A_HEADS
    tk = min(vt_all.shape[-1], kv_rows)
    nkv = kv_rows // tk
    rows = A_HEADS * tq
    kern = functools.partial(_attn_a_kernel, tq=tq, tk=tk, nkv=nkv, use_rope=use_rope)
    return pl.pallas_call(
        kern,
        grid=(b, s // tq),
        in_specs=[pl.BlockSpec((1, tq, 512), lambda bb, i: (bb, i, P2_QA // 512)),
                  pl.BlockSpec((1, kv_rows, LANES), lambda bb, i: (bb, 0, 0)),
                  pl.BlockSpec((1, nkv, LANES, tk), lambda bb, i: (bb, 0, 0, 0)),
                  pl.BlockSpec((tq, LANES), lambda bb, i: (i, 0)),
                  pl.BlockSpec((tq, LANES), lambda bb, i: (i, 0)),
                  pl.BlockSpec((1, LANES), lambda bb, i: (0, 0))],
        out_specs=pl.BlockSpec((1, tq, 512), lambda bb, i: (bb, i, 0)),
        out_shape=jax.ShapeDtypeStruct((b, s, 512), BF16),
        scratch_shapes=[pltpu.VMEM((rows, LANES), BF16)],
        compiler_params=_cparams(("arbitrary", "arbitrary")),
        name="attn_gqa",
    )(pb_q, k_all, vt_all, cos_t, sin_t, gq128)


def _attn_c_kernel(q_ref, k_ref, vt_ref, cos_ref, sin_ref, dl_ref, sub_ref, o_ref, qs_ref,
                   *, tq, tk, nkv, use_rope, lam_init):
    lo = _lane_iota((tq, LANES)) < HEAD_DIM
    x = q_ref[0].astype(F32)
    if use_rope:
        x = _rope128(x, cos_ref[...], sin_ref[...])
    x = x * QSCALE
    qs_ref[0:tq, :] = jnp.where(lo, x, 0.0).astype(BF16)
    qs_ref[tq:2 * tq, :] = jnp.where(lo, 0.0, x).astype(BF16)

    ls, accs = _flash_t(qs_ref, k_ref, lambda c, n: vt_ref[0, 0, c], tk, nkv)

    dl = dl_ref[...]
    lam = (jnp.exp(jnp.sum(dl[0:1] * dl[1:2], axis=1, keepdims=True))
           - jnp.exp(jnp.sum(dl[2:3] * dl[3:4], axis=1, keepdims=True)) + lam_init)
    o = jnp.concatenate([acc * (1.0 / l) for acc, l in zip(accs, ls)], axis=1)
    y = (o[:, 0:tq] - lam * o[:, tq:2 * tq]).T
    ms = jnp.mean(y * y, axis=1, keepdims=True)
    y = ((y * lax.rsqrt(ms + RMS_EPS)) * sub_ref[...]) * (1.0 - lam_init)
    o_ref[0] = y.astype(BF16)


def _attn_c(pb_q, k_all, vt_all, cos_t, sin_t, dlam, subln, kv_rows, use_rope, lam_init):
    b, s, _ = pb_q.shape
    tq = min(QSTRIP, s)
    tk = min(vt_all.shape[-1], kv_rows)
    nkv = kv_rows // tk
    rows = 2 * tq
    kern = functools.partial(_attn_c_kernel, tq=tq, tk=tk, nkv=nkv, use_rope=use_rope, lam_init=lam_init)
    return pl.pallas_call(
        kern,
        grid=(b, C_HEADS, s // tq),
        in_specs=[pl.BlockSpec((1, tq, LANES), lambda bb, h, i: (bb, i, P2_QC // LANES + h)),
                  pl.BlockSpec((1, kv_rows, LANES), lambda bb, h, i: (bb, 0, h)),
                  pl.BlockSpec((1, 1, nkv, LANES, tk), lambda bb, h, i: (bb, h, 0, 0, 0)),
                  pl.BlockSpec((tq, LANES), lambda bb, h, i: (i, 0)),
                  pl.BlockSpec((tq, LANES), lambda bb, h, i: (i, 0)),
                  pl.BlockSpec((4, HEAD_DIM), lambda bb, h, i: (0, 0)),
                  pl.BlockSpec((1, LANES), lambda bb, h, i: (0, 0))],
        out_specs=pl.BlockSpec((1, tq, LANES), lambda bb, h, i: (bb, i, h)),
        out_shape=jax.ShapeDtypeStruct((b, s, 512), BF16),
        scratch_shapes=[pltpu.VMEM((rows, LANES), BF16)],
        compiler_params=_cparams(("arbitrary", "arbitrary", "arbitrary")),
        name="attn_diff",
    )(pb_q, k_all, vt_all, cos_t, sin_t, dlam, subln)


def _gelu_tanh(x):
    return 0.5 * x * (1.0 + jnp.tanh(math.sqrt(2.0 / math.pi) * (x + 0.044715 * (x * x * x))))


def _lru_kernel(*refs, nb, t, nchunks, reverse, final):
    (x_ref, xp_ref, xn_ref, h0_ref, cw_ref, cb_ref, wa_ref, ba_ref, wx_ref, bx_ref, lam_ref) = refs[:11]
    if final:
        gb_ref, hp_ref, y_ref, hl_ref, a_s, b_s, o_s, h_s = refs[11:]
    else:
        y_ref, hl_ref, a_s, b_s, o_s, h_s = refs[11:]
    c = pl.program_id(0)
    pos = (nchunks - 1 - c) if reverse else c

    @pl.when(c == 0)
    def _():
        h_s[...] = h0_ref[...]

    z = -lam_ref[...]
    softplus = jnp.maximum(z, 0.0) + jnp.log(1.0 + jnp.exp(-jnp.abs(z)))
    cw = cw_ref[...]
    for b in range(nb):
        x = x_ref[b]
        prev = jnp.where(pos == 0, 0.0, xp_ref[b])
        nxt = jnp.where(pos == nchunks - 1, 0.0, xn_ref[b])
        ext = jnp.concatenate([prev, x, nxt], axis=0)
        u = (cw[0:1] * ext[6:6 + t] + cw[1:2] * ext[7:7 + t] + cw[2:3] * x
             + cw[3:4] * ext[9:9 + t]) + cb_ref[...]
        ub = u.astype(BF16)
        r = jax.nn.sigmoid(jnp.dot(ub, wa_ref[...], preferred_element_type=F32) + ba_ref[...])
        gi = jax.nn.sigmoid(jnp.dot(ub, wx_ref[...], preferred_element_type=F32) + bx_ref[...])
        log_a = (-LRU_C * r) * softplus
        a = jnp.exp(log_a)
        mult = jnp.sqrt(-jnp.tanh(log_a) * (a * a + 1.0))
        a_s[b] = a
        b_s[b] = mult * (gi * u)

    ngroups = t // SUBLANES
    sub = lax.broadcasted_iota(I32, (SUBLANES, LRU_WIDTH), 0)
    order = range(SUBLANES - 1, -1, -1) if reverse else range(SUBLANES)

    def group(gidx, hs):
        g = (ngroups - 1 - gidx) if reverse else gidx
        base = pl.multiple_of(g * SUBLANES, SUBLANES)
        new = []
        for b in range(nb):
            ta = a_s[b, pl.ds(base, SUBLANES), :]
            tb = b_s[b, pl.ds(base, SUBLANES), :]
            h = hs[b]
            out = jnp.zeros((SUBLANES, LRU_WIDTH), F32)
            for r in order:
                h = (jnp.broadcast_to(ta[r:r + 1], h.shape) * h
                     + jnp.broadcast_to(tb[r:r + 1], h.shape))
                out = jnp.where(sub == r, h, out)
            o_s[b, pl.ds(base, SUBLANES), :] = out
            new.append(h)
        return tuple(new)

    hs = lax.fori_loop(0, ngroups, group, tuple(h_s[b] for b in range(nb)))
    for b in range(nb):
        h_s[b] = hs[b]
        if final:
            y = hp_ref[b] + o_s[b]
            y_ref[b] = (y * _gelu_tanh(gb_ref[b].astype(F32))).astype(y_ref.dtype)
        else:
            y_ref[b] = o_s[b]
    hl_ref[...] = h_s[...]


def _lru_pass(p1, h0, prm, d, reverse, gb_src=None, hprev=None):
    nb, length, _ = p1.shape
    t = min(256, length)
    nchunks = length // t
    final = gb_src is not None
    blk8 = t // SUBLANES
    nblk8 = length // SUBLANES
    xcol = P1_XB // LRU_WIDTH
    cpos = (lambda c: nchunks - 1 - c) if reverse else (lambda c: c)
    vec = lambda: pl.BlockSpec((1, LRU_WIDTH), lambda c: (0, 0))
    mat = lambda: pl.BlockSpec((LRU_WIDTH, LRU_WIDTH), lambda c: (0, 0))
    in_specs = [pl.BlockSpec((nb, t, LRU_WIDTH), lambda c: (0, cpos(c), xcol)),
                pl.BlockSpec((nb, SUBLANES, LRU_WIDTH),
                             lambda c: (0, jnp.maximum(cpos(c) * blk8 - 1, 0), xcol)),
                pl.BlockSpec((nb, SUBLANES, LRU_WIDTH),
                             lambda c: (0, jnp.minimum((cpos(c) + 1) * blk8, nblk8 - 1), xcol)),
                pl.BlockSpec((nb, SUBLANES, LRU_WIDTH), lambda c: (0, 0, 0)),
                pl.BlockSpec((CONV_WIDTH, LRU_WIDTH), lambda c: (0, 0)), vec(),
                mat(), vec(), mat(), vec(), vec()]
    args = [p1, p1, p1, h0, prm["conv_w"], prm["conv_b"], prm["wa"][d], prm["ba"][d],
            prm["wx"][d], prm["bx"][d], prm["lam"][d]]
    if final:
        in_specs += [pl.BlockSpec((nb, t, LRU_WIDTH), lambda c: (0, cpos(c), P2_GB // LRU_WIDTH)),
                     pl.BlockSpec((nb, t, LRU_WIDTH), lambda c: (0, cpos(c), 0))]
        args += [gb_src, hprev]
    kern = functools.partial(_lru_kernel, nb=nb, t=t, nchunks=nchunks, reverse=reverse, final=final)
    return pl.pallas_call(
        kern,
        grid=(nchunks,),
        in_specs=in_specs,
        out_specs=[pl.BlockSpec((nb, t, LRU_WIDTH), lambda c: (0, cpos(c), 0)),
                   pl.BlockSpec((nb, SUBLANES, LRU_WIDTH), lambda c: (0, 0, 0))],
        out_shape=[jax.ShapeDtypeStruct((nb, length, LRU_WIDTH), BF16 if final else F32),
                   jax.ShapeDtypeStruct((nb, SUBLANES, LRU_WIDTH), F32)],
        scratch_shapes=[pltpu.VMEM((nb, t, LRU_WIDTH), F32), pltpu.VMEM((nb, t, LRU_WIDTH), F32),
                        pltpu.VMEM((nb, t, LRU_WIDTH), F32), pltpu.VMEM((nb, SUBLANES, LRU_WIDTH), F32)],
        compiler_params=_cparams(("arbitrary",)),
        name="lru_bwd" if reverse else "lru_fwd",
    )(*args)


def _merge_kernel(ya_ref, yb_ref, yc_ref, gm_ref, x_ref, g1_ref, sc2_ref, sh2_ref, lg_ref, lb_ref,
                  wb_ref, wo_ref, wr_ref, x1_ref, h2_ref, aff_ref):
    acc = None
    for n, br in enumerate((ya_ref, yb_ref, yc_ref)):
        p = jnp.dot(br[0], wb_ref[n], preferred_element_type=F32)
        gate = jax.nn.sigmoid(gm_ref[0, :, n * D_MODEL:(n + 1) * D_MODEL])
        acc = gate * p if acc is None else acc + gate * p
    o = jnp.dot(acc.astype(BF16), wo_ref[...], preferred_element_type=F32)
    x1 = _layer_norm(DN_ALPHA * x_ref[0] + g1_ref[0] * o, lg_ref[...], lb_ref[...])
    x1_ref[0] = x1
    h2 = x1 * (1.0 + sc2_ref[0]) + sh2_ref[0]
    h2_ref[0] = h2
    logits = jnp.dot(h2.astype(BF16), wr_ref[...], preferred_element_type=F32)
    logits = jnp.where(_lane_iota(logits.shape) < N_EXPERTS, logits, -jnp.inf)
    e = jnp.exp(logits - jnp.max(logits, axis=-1, keepdims=True))
    aff_ref[0] = e / jnp.sum(e, axis=-1, keepdims=True)


def _merge(ya, yb, yc, p1, x, g1, sc2, sh2, lg, lb, wb, wo, wr):
    b, s, d = x.shape
    tm = min(512, s)
    row = lambda w, col=0: pl.BlockSpec((1, tm, w), lambda bb, i: (bb, i, col))
    per_b = lambda: pl.BlockSpec((1, 1, d), lambda bb, i: (bb, 0, 0))
    full = lambda shape: pl.BlockSpec(shape, lambda bb, i: (0,) * len(shape))
    return pl.pallas_call(
        _merge_kernel,
        grid=(b, s // tm),
        in_specs=[row(512), row(512), row(512), row(N_BRANCH * d, P1_GM // (N_BRANCH * d)), row(d),
                  per_b(), per_b(), per_b(), full((1, d)), full((1, d)),
                  full((N_BRANCH, BRANCH_W, d)), full((d, d)), full((d, LANES))],
        out_specs=[row(d), row(d), row(LANES)],
        out_shape=[jax.ShapeDtypeStruct((b, s, d), F32), jax.ShapeDtypeStruct((b, s, d), F32),
                   jax.ShapeDtypeStruct((b, s, LANES), F32)],
        compiler_params=_cparams(("arbitrary", "arbitrary")),
        name="merge_ln1_router",
    )(ya, yb, yc, p1, x, g1, sc2, sh2, lg, lb, wb, wo, wr)


def _route_kernel(aff_ref, pos_ref, base_ref, cnt_ref, x_s, c_s, *, n, cap, blk):
    v = aff_ref[0]
    kf = float(cap)

    def bit_body(i, thr):
        cand = thr | lax.shift_left(jnp.int32(1), 30 - i)
        hit = jnp.where(v >= pltpu.bitcast(cand, F32), 1.0, 0.0)
        return jnp.where(jnp.sum(hit, axis=0, keepdims=True) >= kf, cand, thr)

    thr = pltpu.bitcast(lax.fori_loop(0, 31, bit_body, jnp.zeros((1, LANES), I32)), F32)
    gt = v > thr
    eq = v == thr
    need = kf - jnp.sum(jnp.where(gt, 1.0, 0.0), axis=0, keepdims=True)

    nblk = n // blk
    tri = (lax.broadcasted_iota(I32, (blk, blk), 0) > lax.broadcasted_iota(I32, (blk, blk), 1))
    tri = jnp.where(tri, 1.0, 0.0).astype(BF16)

    def excl_cumsum(write_base):
        carry = jnp.zeros((1, LANES), F32)
        for j in range(nblk):
            xb = x_s[j * blk:(j + 1) * blk, :]
            c_s[j * blk:(j + 1) * blk, :] = (
                jnp.dot(tri, xb.astype(BF16), preferred_element_type=F32) + carry)
            if write_base:
                base_ref[0, j:j + 1, :] = carry.astype(I32)
            carry = carry + jnp.sum(xb, axis=0, keepdims=True)

    x_s[...] = jnp.where(eq, 1.0, 0.0)
    excl_cumsum(False)
    take_eq = jnp.where(c_s[...] < need, 1.0, 0.0)
    sel = jnp.where(gt, 1.0, jnp.where(eq, take_eq, 0.0))
    sel = jnp.where(_lane_iota(sel.shape) < N_EXPERTS, sel, 0.0)
    x_s[...] = sel
    excl_cumsum(True)
    pos_ref[0] = jnp.where(x_s[...] > 0.5, c_s[...].astype(I32), -1)
    cnt_ref[0] = (c_s[...] + x_s[...]).astype(I32)


def _route(aff, cap, blk):
    b, n, _ = aff.shape
    kern = functools.partial(_route_kernel, n=n, cap=cap, blk=blk)
    tok = lambda: pl.BlockSpec((1, n, LANES), lambda bb: (bb, 0, 0))
    return pl.pallas_call(
        kern,
        grid=(b,),
        in_specs=[tok()],
        out_specs=[tok(), pl.BlockSpec((1, n // blk, LANES), lambda bb: (bb, 0, 0)), tok()],
        out_shape=[jax.ShapeDtypeStruct((b, n, LANES), I32),
                   jax.ShapeDtypeStruct((b, n // blk, LANES), I32),
                   jax.ShapeDtypeStruct((b, n, LANES), I32)],
        scratch_shapes=[pltpu.VMEM((n, LANES), F32), pltpu.VMEM((n, LANES), F32)],
        compiler_params=_cparams(("arbitrary",)),
        name="route",
    )(aff)


def _slot_index_kernel(base_ref, cnt_ref, idx_ref, acc_s, add_s, *, nblk, cap):
    b = pl.program_id(0)
    ncol = max(cap // LANES, 1)
    pw = min(cap, LANES)
    lane = _lane_iota((LANES, LANES))
    for e in range(N_EXPERTS):
        acc_s[...] = jnp.zeros(acc_s.shape, F32)
        add_s[...] = jnp.zeros(add_s.shape, F32)

        def block(r, carry, e=e):
            k0 = lax.shift_right_logical(base_ref[(b * nblk + r) * N_EXPERTS + e], 7)
            off = pl.multiple_of(r * LANES, LANES)
            cb = jnp.broadcast_to(cnt_ref[0, pl.ds(off, LANES), e:e + 1], (LANES, LANES))
            for dk in range(2):
                hit = jnp.where(cb <= lane + (k0 + dk) * LANES, 1.0, 0.0)
                acc_s[k0 + dk] += jnp.sum(hit.reshape(-1, SUBLANES, LANES), axis=0)
            add_s[k0 + 2] += float(LANES // SUBLANES)
            return carry

        lax.fori_loop(0, nblk, block, 0, unroll=min(8, nblk))
        run = jnp.zeros((SUBLANES, LANES), F32)
        for k in range(ncol):
            run = run + add_s[k]
            tot = jnp.sum(acc_s[k] + run, axis=0, keepdims=True)
            idx_ref[0, e:e + 1, LANES * k:LANES * k + pw] = tot[:, :pw].astype(I32)


def _slot_index(base_flat, cnt, cap):
    b, n, _ = cnt.shape
    nblk = n // LANES
    ncol = max(cap // LANES, 1)
    kern = functools.partial(_slot_index_kernel, nblk=nblk, cap=cap)
    grid_spec = pltpu.PrefetchScalarGridSpec(
        num_scalar_prefetch=1,
        grid=(b,),
        in_specs=[pl.BlockSpec((1, n, LANES), lambda bb, base: (bb, 0, 0))],
        out_specs=pl.BlockSpec((1, N_EXPERTS, cap), lambda bb, base: (bb, 0, 0)),
        scratch_shapes=[pltpu.VMEM((ncol + 2, SUBLANES, LANES), F32),
                        pltpu.VMEM((ncol + 3, SUBLANES, LANES), F32)],
    )
    return pl.pallas_call(
        kern,
        grid_spec=grid_spec,
        out_shape=jax.ShapeDtypeStruct((b, N_EXPERTS, cap), I32),
        compiler_params=_cparams(("arbitrary",)),
        name="slot_index",
    )(base_flat, cnt)


def _expert_kernel(idx_ref, h2_hbm, wg_ref, wu_ref, wd_ref, y_ref, xbuf, sem, *, nb, ne, cap, tsub):
    e = pl.program_id(0)
    b = pl.program_id(1)
    step = e * nb + b
    slot = step % 2

    def gather(st, sl, r0, r1):
        bb = st % nb
        off = (bb * ne + st // nb) * cap

        def body(r, carry):
            tok = idx_ref[off + r]
            pltpu.make_async_copy(h2_hbm.at[bb, pl.ds(tok, 1), :], xbuf.at[sl, pl.ds(r, 1), :],
                                  sem.at[sl]).start()
            return carry

        lax.fori_loop(r0, r1, body, 0, unroll=8)

    @pl.when(step == 0)
    def _():
        gather(step, slot, 0, cap)

    @pl.when(step + 1 < nb * ne)
    def _():
        gather(step + 1, 1 - slot, 0, cap)

    pltpu.make_async_copy(xbuf.at[slot], xbuf.at[slot], sem.at[slot]).wait()

    for s in range(cap // tsub):
        x = xbuf[slot, s * tsub:(s + 1) * tsub, :].astype(BF16)
        g = jnp.dot(x, wg_ref[0, 0], preferred_element_type=F32)
        u = jnp.dot(x, wu_ref[0, 0], preferred_element_type=F32)
        act = ((g * jax.nn.sigmoid(g)) * u).astype(BF16)
        y_ref[0, 0, s * tsub:(s + 1) * tsub, :] = jnp.dot(
            act, wd_ref[0, 0], preferred_element_type=F32).astype(y_ref.dtype)


def _experts(idx_flat, h2, wg, wu, wd, l, cap):
    nb, n, d = h2.shape
    _, ne, _, fp = wg.shape
    tsub = min(256, cap)
    kern = functools.partial(_expert_kernel, nb=nb, ne=ne, cap=cap, tsub=tsub)
    grid_spec = pltpu.PrefetchScalarGridSpec(
        num_scalar_prefetch=1,
        grid=(ne, nb),
        in_specs=[pl.BlockSpec(memory_space=pl.ANY),
                  pl.BlockSpec((1, 1, d, fp), lambda e, b, idx: (l, e, 0, 0)),
                  pl.BlockSpec((1, 1, d, fp), lambda e, b, idx: (l, e, 0, 0)),
                  pl.BlockSpec((1, 1, fp, d), lambda e, b, idx: (l, e, 0, 0))],
        out_specs=pl.BlockSpec((1, 1, cap, d), lambda e, b, idx: (b, e, 0, 0)),
        scratch_shapes=[pltpu.VMEM((2, cap, d), F32), pltpu.SemaphoreType.DMA((2,))],
    )
    return pl.pallas_call(
        kern,
        grid_spec=grid_spec,
        out_shape=jax.ShapeDtypeStruct((nb, ne, cap, d), BF16),
        compiler_params=_cparams(("arbitrary", "arbitrary"), 60 * 1024 * 1024),
        name="expert_ffn",
    )(idx_flat, h2, wg, wu, wd)


def _combine_kernel(base_ref, x1_ref, pos_ref, aff_ref, g2_ref, lg_ref, lb_ref, y_hbm, o_ref, ybuf, sem,
                    *, nb, nt, ne, cap, tt, win):
    b = pl.program_id(0)
    i = pl.program_id(1)
    step = b * nt + i
    slot = step % 2

    def win_start(st, e):
        base = base_ref[st * ne + e]
        al = lax.shift_left(lax.shift_right_logical(base, 4), 4)
        return pl.multiple_of(jnp.minimum(al, cap - win), BF16_ROWS)

    def fetch(st, sl):
        bb = st // nt
        for e in range(ne):
            pltpu.make_async_copy(y_hbm.at[bb, e, pl.ds(win_start(st, e), win), :], ybuf.at[sl, e],
                                  sem.at[sl]).start()

    @pl.when(step == 0)
    def _():
        fetch(step, slot)

    @pl.when(step + 1 < nb * nt)
    def _():
        fetch(step + 1, 1 - slot)

    for e in range(ne):
        pltpu.make_async_copy(y_hbm.at[0, e, pl.ds(0, win), :], ybuf.at[slot, e], sem.at[slot]).wait()

    pos = pos_ref[0]
    aff = aff_ref[0]
    col = _lane_iota((tt, win))
    acc = jnp.zeros((tt, D_MODEL), F32)
    for e in range(ne):
        rel = pos[:, e:e + 1] - win_start(step, e)
        onehot = jnp.where(col == rel, 1.0, 0.0).astype(BF16)
        acc = acc + aff[:, e:e + 1] * jnp.dot(onehot, ybuf[slot, e], preferred_element_type=F32)
    o_ref[0] = _layer_norm(DN_ALPHA * x1_ref[0] + g2_ref[0] * acc, lg_ref[...], lb_ref[...])


def _combine(base_flat, x1, pos, aff, g2, lg, lb, y, tt):
    nb, n, d = x1.shape
    ne, cap = y.shape[1], y.shape[2]
    nt = n // tt
    win = min(tt + BF16_ROWS, cap)
    kern = functools.partial(_combine_kernel, nb=nb, nt=nt, ne=ne, cap=cap, tt=tt, win=win)
    row = lambda w: pl.BlockSpec((1, tt, w), lambda b, i, base: (b, i, 0))
    grid_spec = pltpu.PrefetchScalarGridSpec(
        num_scalar_prefetch=1,
        grid=(nb, nt),
        in_specs=[row(d), row(LANES), row(LANES),
                  pl.BlockSpec((1, 1, d), lambda b, i, base: (b, 0, 0)),
                  pl.BlockSpec((1, d), lambda b, i, base: (0, 0)),
                  pl.BlockSpec((1, d), lambda b, i, base: (0, 0)),
                  pl.BlockSpec(memory_space=pl.ANY)],
        out_specs=row(d),
        scratch_shapes=[pltpu.VMEM((2, ne, win, d), BF16), pltpu.SemaphoreType.DMA((2,))],
    )
    return pl.pallas_call(
        kern,
        grid_spec=grid_spec,
        out_shape=jax.ShapeDtypeStruct((nb, n, d), F32),
        compiler_params=_cparams(("arbitrary", "arbitrary")),
        name="combine_ln2",
    )(base_flat, x1, pos, aff, g2, lg, lb, y)


def _moe(x1, h2, aff, g2, lg, lb, wg, wu, wd, l):
    nb, n, d = x1.shape
    cap = EC_FACTOR * n // N_EXPERTS
    tt = min(128, n)
    assert tt == min(LANES, n)
    pos, base, cnt = _route(aff, cap, tt)
    base_flat = base[:, :, :N_EXPERTS].reshape(-1)
    idx = _slot_index(base_flat, cnt, cap)
    y = _experts(idx.reshape(-1), h2, wg, wu, wd, l, cap)
    return _combine(base_flat, x1, pos, aff, g2, lg, lb, y, tt)


def _rope_tables(seq):
    rows = seq // GRID_W
    row = jnp.repeat(jnp.arange(rows, dtype=I32), GRID_W)
    colm = jnp.tile(jnp.arange(GRID_W, dtype=I32), rows)
    inv = ROPE_THETA ** (-jnp.arange(N_FREQ, dtype=F32) / N_FREQ)
    ang = jnp.stack([row, colm], -1).astype(F32)[..., None] * inv
    cos, sin = jnp.cos(ang), jnp.sin(ang)
    c64 = jnp.concatenate([cos[:, 0], cos[:, 0], cos[:, 1], cos[:, 1]], axis=-1)
    s64 = jnp.concatenate([-sin[:, 0], sin[:, 0], -sin[:, 1], sin[:, 1]], axis=-1)
    return jnp.tile(c64, (1, 2)), jnp.tile(s64, (1, 2))


def _block_diag(w):
    nblk, bw, _ = w.shape
    eye = jnp.eye(nblk, dtype=w.dtype)
    return (eye[:, None, :, None] * w[:, :, None, :]).reshape(nblk * bw, nblk * bw)


def kernel(x, c, ctx, c_ctx, w_ada, b_ada, w_in, a_q_norm, a_k_norm, lru_conv_w, lru_conv_b, lru_w_a,
           lru_b_a, lru_w_x, lru_b_x, lru_lambda, diff_lambda, diff_subln, w_branch, w_out, ln1_g, ln1_b,
           w_router, w_gate, w_up, w_down, ln2_g, ln2_b):
    nb, seq, d = x.shape
    lc = ctx.shape[1]
    depth = w_in.shape[0]
    assert d == D_MODEL and nb + 1 <= SUBLANES and seq % lc == 0 and lc % LANES == 0

    cos_t, sin_t = _rope_tables(seq)
    cc = jnp.zeros((SUBLANES, d), F32).at[:nb].set(c).at[nb].set(c_ctx)
    mod_all = _ada(cc, w_ada, b_ada)

    zeros8 = jnp.zeros((nb, SUBLANES, LRU_WIDTH), F32)

    w1 = jnp.concatenate([w_in[:, :, 3328:6400], w_in[:, :, 768:1280]], axis=2).astype(BF16)
    w2 = jnp.concatenate([w_in[:, :, 0:512], w_in[:, :, 1280:1792], w_in[:, :, 1792:2304],
                          w_in[:, :, 2304:2816], w_in[:, :, 2816:3328], w_in[:, :, 512:640],
                          w_in[:, :, 640:768]], axis=2).astype(BF16)
    wg, wu, wd = w_gate.astype(BF16), w_up.astype(BF16), w_down.astype(BF16)

    for l in range(depth):
        need_ctx = l < depth - 1
        lam_init = 0.8 - 0.6 * math.exp(-0.3 * l)
        mod = mod_all[l]
        m6 = mod[:nb].reshape(nb, 6, 1, d)
        sh1, sc1, g1, sh2, sc2, g2 = (m6[:, k] for k in range(6))
        mc6 = jnp.broadcast_to(mod[nb].reshape(1, 6, 1, d), (nb, 6, 1, d))
        sh1c, sc1c, g1c, sh2c, sc2c, g2c = (mc6[:, k] for k in range(6))

        gq128 = jnp.tile(a_q_norm[l], 2).reshape(1, LANES)
        gk128 = jnp.tile(a_k_norm[l], 2).reshape(1, LANES)
        lru = {
            "conv_w": lru_conv_w[l], "conv_b": lru_conv_b[l].reshape(1, -1),
            "wa": [_block_diag(lru_w_a[l, k]).astype(BF16) for k in range(2)],
            "wx": [_block_diag(lru_w_x[l, k]).astype(BF16) for k in range(2)],
            "ba": [lru_b_a[l, k].reshape(1, -1) for k in range(2)],
            "bx": [lru_b_x[l, k].reshape(1, -1) for k in range(2)],
            "lam": [lru_lambda[l, k].reshape(1, -1) for k in range(2)],
        }
        subln = diff_subln[l].reshape(1, LANES)
        wb = w_branch[l].astype(BF16)
        wo = w_out[l].astype(BF16)
        wr = jnp.zeros((d, LANES), F32).at[:, :N_EXPERTS].set(w_router[l]).astype(BF16)
        lg1, lb1 = ln1_g[l].reshape(1, d), ln1_b[l].reshape(1, d)
        lg2, lb2 = ln2_g[l].reshape(1, d), ln2_b[l].reshape(1, d)

        p1 = _mod_matmul(x, sc1, sh1, w1, l, F32, P1_W, "inproj_f32")
        p2 = _mod_matmul(x, sc1, sh1, w2, l, BF16, P2_W, "inproj_bf16")
        p1c = _mod_matmul(ctx, sc1c, sh1c, w1, l, F32, P1_W, "inproj_f32")
        p2c = _mod_matmul(ctx, sc1c, sh1c, w2, l, BF16, P2_W, "inproj_bf16")

        kv = lc + seq
        ka, vat, kc, vct = _prep_kv(p2, p2c, cos_t, sin_t, gk128, _pick_tk(kv))
        ya = _attn_a(p2, ka, vat, cos_t, sin_t, gq128, kv, True)
        yc = _attn_c(p2, kc, vct, cos_t, sin_t, diff_lambda[l], subln, kv, True, lam_init)

        hf_c, hl_f = _lru_pass(p1c, zeros8, lru, 0, False)
        yb_c, hl_b = _lru_pass(p1c, zeros8, lru, 1, True, gb_src=p2c, hprev=hf_c)
        hf, _ = _lru_pass(p1, hl_f, lru, 0, False)
        yb, _ = _lru_pass(p1, hl_b, lru, 1, True, gb_src=p2, hprev=hf)

        x1, h2, aff = _merge(ya, yb, yc, p1, x, g1, sc2, sh2, lg1, lb1, wb, wo, wr)
        x_next = _moe(x1, h2, aff, g2, lg2, lb2, wg, wu, wd, l)

        if need_ctx:
            ya_c = _attn_a(p2c, ka, vat, cos_t, sin_t, gq128, lc, False)
            yc_c = _attn_c(p2c, kc, vct, cos_t, sin_t, diff_lambda[l], subln, lc, False, lam_init)
            c1, hc2, affc = _merge(ya_c, yb_c, yc_c, p1c, ctx, g1c, sc2c, sh2c, lg1, lb1, wb, wo, wr)
            ctx = _moe(c1, hc2, affc, g2c, lg2, lb2, wg, wu, wd, l)
        x = x_next
    return x
```
